```python
import math
import jax, jax.numpy as jnp
from jax import lax
import numpy as np

D_MODEL = 4096
BATCH = 4
SEQ = 2048
DEPTH = 2
DEC_BATCH = 128
DEC_SEQ = 1
PAST_LEN = 16384
PAGE_SIZE = 128

D_RWKV = D_MODEL // 2
D_HGRN = D_MODEL - D_RWKV
RWKV_HEAD = 64
RWKV_HEADS = D_RWKV // RWKV_HEAD
HGRN_HEAD = 128
HGRN_HEADS = D_HGRN // HGRN_HEAD
DECAY_RANK = max(32, int(round(1.8 * D_RWKV ** 0.5 / 32)) * 32)
AAA_RANK = max(32, int(round(1.8 * D_RWKV ** 0.5 / 32)) * 32)
MV_RANK = max(32, int(round(1.3 * D_RWKV ** 0.5 / 32)) * 32)
GATE_RANK = max(32, int(round(0.6 * D_RWKV ** 0.8 / 32)) * 32)
RWKV_PROJ = 3 * D_RWKV + DECAY_RANK + AAA_RANK + GATE_RANK
PROJ = RWKV_PROJ + 4 * D_HGRN
D_FF = -(-(8 * D_MODEL) // (3 * 256)) * 256
HGRN_CHUNK = 64
EPS_RMS = 1e-6
EPS_LNX = 64e-5
EPS_HGRN = 1e-5
MIN_FORGET = 1e-30

kernel_name = 'hybrid_rwkv7_hgrn2_step'

F32 = jnp.float32


def rms_norm(x, g, eps=EPS_RMS):
    xf = x.astype(F32)
    y = xf * lax.rsqrt(jnp.mean(xf * xf, axis=-1, keepdims=True) + eps)
    return (y * g.astype(F32)).astype(x.dtype)


def heads(t, n):
    return t.reshape(t.shape[:-1] + (n, t.shape[-1] // n))


def lower_bounds(lb_logits):
    p = jax.nn.softmax(lb_logits.astype(F32), axis=0)
    return jnp.cumsum(p, axis=0) - p[0:1]


def wkv7_scan(r, decay, k, v, kk, a, s0):
    def step(S, inp):
        r_t, w_t, k_t, v_t, kk_t, a_t = inp
        sa = jnp.einsum('bhij,bhj->bhi', S, kk_t)
        S = (S * w_t[:, :, None, :]
             - sa[..., None] * (kk_t * a_t)[:, :, None, :]
             + v_t[..., None] * k_t[:, :, None, :])
        y_t = jnp.einsum('bhij,bhj->bhi', S, r_t)
        return S, y_t
    xs = tuple(jnp.moveaxis(t, 1, 0) for t in (r, decay, k, v, kk, a))
    s_last, ys = lax.scan(step, s0, xs)
    return jnp.moveaxis(ys, 0, 1), s_last


def hgrn2_chunked(q, k, v, logf, s0):
    B, T, H, DK = q.shape
    DV = v.shape[-1]
    C = min(HGRN_CHUNK, T)
    n = -(-T // C)
    pad = n * C - T

    def blocks(t):
        t = jnp.pad(t, ((0, 0), (0, pad), (0, 0), (0, 0)))
        return t.reshape(B, n, C, H, t.shape[-1]).transpose(1, 0, 3, 2, 4)

    causal = jnp.tril(jnp.ones((C, C), dtype=bool))[:, :, None]

    def step(S, inp):
        qc, kc, vc, gc = inp
        b = jnp.cumsum(gc, axis=2)
        diff = b[:, :, :, None, :] - b[:, :, None, :, :]
        dec = jnp.where(causal, jnp.exp(jnp.where(causal, diff, 0.0)), 0.0)
        A = jnp.einsum('bhtd,bhsd,bhtsd->bhts', qc, kc, dec)
        o = (jnp.einsum('bhtd,bhdv->bhtv', qc * jnp.exp(b), S)
             + jnp.einsum('bhts,bhsv->bhtv', A, vc))
        b_last = b[:, :, -1:, :]
        S = (S * jnp.exp(b_last[:, :, 0, :])[..., None]
             + jnp.einsum('bhsd,bhsv->bhdv', kc * jnp.exp(b_last - b), vc))
        return S, o

    s_last, os_ = lax.scan(step, s0, tuple(blocks(t) for t in (q, k, v, logf)))
    o = os_.transpose(1, 0, 3, 2, 4).reshape(B, n * C, H, DV)[:, :T]
    return o, s_last


def mixer(h, shift0, wkv0, hgrn0, v_first, l, p):
    proj = h @ p['w_in'][l]
    pr, ph = proj[..., :RWKV_PROJ], proj[..., RWKV_PROJ:]
    prev = jnp.concatenate([shift0[:, None].astype(pr.dtype), pr[:, :-1]], axis=1)
    xs = (pr + (prev - pr) * p['mu_shift'][l]).astype(F32)
    cuts = [int(c) for c in np.cumsum([D_RWKV, D_RWKV, D_RWKV, DECAY_RANK, AAA_RANK])]
    xr, xk, xv, xw, xa, xg = jnp.split(xs, cuts, axis=-1)
    w_log = -jax.nn.softplus(-(p['w0'][l] + jnp.tanh(xw) @ p['w2'][l])) - 0.5
    decay = jnp.exp(-jnp.exp(w_log))
    a = jax.nn.sigmoid(p['a0'][l] + xa @ p['a2'][l])
    g = jax.nn.sigmoid(xg) @ p['g2'][l]
    v = xv
    if l == 0:
        v_first = v
    else:
        v_gate = jax.nn.sigmoid(p['v0'][l - 1] + (xv @ p['v1'][l - 1]) @ p['v2'][l - 1])
        v = v + (v_first - v) * v_gate
    kk = heads(xk * p['k_k'][l], RWKV_HEADS)
    kk = kk / jnp.maximum(jnp.sqrt(jnp.sum(kk * kk, axis=-1, keepdims=True)), 1e-12)
    k = xk * (1.0 + (a - 1.0) * p['k_a'][l])
    rH, kH, vH = heads(xr, RWKV_HEADS), heads(k, RWKV_HEADS), heads(v, RWKV_HEADS)
    y, wkv_new = wkv7_scan(rH, heads(decay, RWKV_HEADS), kH, vH, kk,
                           heads(a, RWKV_HEADS), wkv0.astype(F32))
    mean = jnp.mean(y, axis=-1, keepdims=True)
    var = jnp.mean(jnp.square(y - mean), axis=-1, keepdims=True)
    yn = (y - mean) * lax.rsqrt(var + EPS_LNX)
    yn = yn * heads(p['lnx_w'][l], RWKV_HEADS) + heads(p['lnx_b'][l], RWKV_HEADS)
    bonus = jnp.sum(rH * kH * p['r_k'][l], axis=-1, keepdims=True) * vH
    y_rwkv = (yn + bonus).reshape(y.shape[:2] + (D_RWKV,)) * g
    qh, fh, ih, gh = jnp.split(ph.astype(F32), 4, axis=-1)
    lb = p['lb'][l]
    f = lb + (1.0 - lb) * jax.nn.sigmoid(fh)
    logf = jnp.log(jnp.maximum(f, MIN_FORGET))
    k_in = (1.0 - lb) * jax.nn.sigmoid(-fh)
    q = jax.nn.silu(qh)
    o, hgrn_new = hgrn2_chunked(heads(q, HGRN_HEADS), heads(k_in, HGRN_HEADS),
                                heads(ih, HGRN_HEADS), heads(logf, HGRN_HEADS), hgrn0.astype(F32))
    o = o * lax.rsqrt(jnp.mean(o * o, axis=-1, keepdims=True) + EPS_HGRN)
    o = o * heads(p['hgrn_norm'][l].astype(F32), HGRN_HEADS)
    y_hgrn = o.reshape(o.shape[:2] + (D_HGRN,)) * jax.nn.silu(gh)
    out = jnp.concatenate([y_rwkv, y_hgrn], axis=-1).astype(h.dtype) @ p['w_out'][l]
    return out, pr[:, -1], wkv_new, hgrn_new, v_first


def trunk(x, wkv_s, hgrn_s, shift_s, p):
    new_wkv, new_hgrn, new_shift = [], [], []
    v_first = None
    for l in range(DEPTH):
        h = rms_norm(x, p['norm_mix'][l])
        m, sh, sw, sg, v_first = mixer(h, shift_s[l], wkv_s[l], hgrn_s[l], v_first, l, p)
        x = x + m.astype(x.dtype)
        h = rms_norm(x, p['norm_ffn'][l])
        ff = (jax.nn.silu(h @ p['w_gate'][l]) * (h @ p['w_up'][l])) @ p['w_down'][l]
        x = x + ff.astype(x.dtype)
        new_wkv.append(sw.astype(wkv_s.dtype))
        new_hgrn.append(sg.astype(hgrn_s.dtype))
        new_shift.append(sh.astype(shift_s.dtype))
    y = rms_norm(x, p['norm_final'])
    return y, jnp.stack(new_wkv), jnp.stack(new_hgrn), jnp.stack(new_shift)


def setup_inputs(seed: int = 0) -> dict:
    key = jax.random.key(seed)
    ks = iter(jax.random.split(key, 40))

    def nrm(shape, scale=1.0, shift=0.0):
        return shift + scale * jax.random.normal(next(ks), shape, F32)

    return {
        'x_prompt': nrm((BATCH, SEQ, D_MODEL)),
        'x_sample': nrm((DEC_BATCH, DEC_SEQ, D_MODEL)),
        'state_wkv': nrm((DEPTH, DEC_BATCH, RWKV_HEADS, RWKV_HEAD, RWKV_HEAD), 0.3),
        'state_hgrn': nrm((DEPTH, DEC_BATCH, HGRN_HEADS, HGRN_HEAD, HGRN_HEAD), 0.3),
        'state_shift': nrm((DEPTH, DEC_BATCH, RWKV_PROJ)),
        'norm_mix': nrm((DEPTH, D_MODEL), 0.02, 1.0),
        'w_in': nrm((DEPTH, D_MODEL, PROJ), D_MODEL ** -0.5),
        'mu_shift': jax.random.uniform(next(ks), (DEPTH, RWKV_PROJ), F32),
        'w0': nrm((DEPTH, D_RWKV), 0.5, -1.0),
        'w2': nrm((DEPTH, DECAY_RANK, D_RWKV), 0.5 * DECAY_RANK ** -0.5),
        'a0': nrm((DEPTH, D_RWKV), 0.5),
        'a2': nrm((DEPTH, AAA_RANK, D_RWKV), 0.5 * AAA_RANK ** -0.5),
        'v0': nrm((DEPTH - 1, D_RWKV), 0.5),
        'v1': nrm((DEPTH - 1, D_RWKV, MV_RANK), D_RWKV ** -0.5),
        'v2': nrm((DEPTH - 1, MV_RANK, D_RWKV), 0.5 * MV_RANK ** -0.5),
        'g2': nrm((DEPTH, GATE_RANK, D_RWKV), GATE_RANK ** -0.5),
        'k_k': nrm((DEPTH, D_RWKV), 0.05, 0.85),
        'k_a': nrm((DEPTH, D_RWKV), 0.05, 1.0),
        'r_k': nrm((DEPTH, RWKV_HEADS, RWKV_HEAD), 0.1),
        'lnx_w': nrm((DEPTH, D_RWKV), 0.02, 1.0),
        'lnx_b': nrm((DEPTH, D_RWKV), 0.02),
        'hgrn_lb': nrm((DEPTH, D_HGRN)),
        'hgrn_norm': nrm((DEPTH, D_HGRN), 0.02, 1.0),
        'w_out': nrm((DEPTH, D_MODEL, D_MODEL), D_MODEL ** -0.5),
        'norm_ffn': nrm((DEPTH, D_MODEL), 0.02, 1.0),
        'w_gate': nrm((DEPTH, D_MODEL, D_FF), D_MODEL ** -0.5),
        'w_up': nrm((DEPTH, D_MODEL, D_FF), D_MODEL ** -0.5),
        'w_down': nrm((DEPTH, D_FF, D_MODEL), D_FF ** -0.5),
        'norm_final': nrm((D_MODEL,), 0.02, 1.0),
    }


def reference(x_prompt, x_sample, state_wkv, state_hgrn, state_shift, norm_mix, w_in, mu_shift,
              w0, w2, a0, a2, v0, v1, v2, g2, k_k, k_a, r_k, lnx_w, lnx_b, hgrn_lb, hgrn_norm,
              w_out, norm_ffn, w_gate, w_up, w_down, norm_final):
    p = dict(norm_mix=norm_mix, w_in=w_in, mu_shift=mu_shift, w0=w0, w2=w2, a0=a0, a2=a2,
             v0=v0, v1=v1, v2=v2, g2=g2, k_k=k_k, k_a=k_a, r_k=r_k, lnx_w=lnx_w, lnx_b=lnx_b,
             lb=lower_bounds(hgrn_lb), hgrn_norm=hgrn_norm, w_out=w_out, norm_ffn=norm_ffn,
             w_gate=w_gate, w_up=w_up, w_down=w_down, norm_final=norm_final)
    b = x_prompt.shape[0]
    zero_wkv = jnp.zeros((DEPTH, b) + state_wkv.shape[2:], state_wkv.dtype)
    zero_hgrn = jnp.zeros((DEPTH, b) + state_hgrn.shape[2:], state_hgrn.dtype)
    zero_shift = jnp.zeros((DEPTH, b) + state_shift.shape[2:], state_shift.dtype)
    y_prompt, wkv_p, hgrn_p, shift_p = trunk(x_prompt, zero_wkv, zero_hgrn, zero_shift, p)
    y_sample, wkv_s, hgrn_s, shift_s = trunk(x_sample, state_wkv, state_hgrn, state_shift, p)
    return (y_prompt, y_sample, wkv_p, hgrn_p, shift_p, wkv_s, hgrn_s, shift_s)
```

```python
import functools

import jax
import jax.numpy as jnp
from jax import lax
from jax.experimental import pallas as pl
from jax.experimental.pallas import tpu as pltpu

F32 = jnp.float32
BF16 = jnp.bfloat16
HI = lax.Precision.HIGHEST

LANES = 128
SUBLANES = 8
VMEM_LIMIT = 56 * 1024 * 1024

CHUNK = 64
SUB = 16
EPS_RMS = 1e-6
EPS_LNX = 64e-5
EPS_HGRN = 1e-5
MIN_FORGET = 1e-30


def _round_up(x, m):
    return -(-x // m) * m


def _pick(n, target, align):
    best = None
    for d in range(align, min(n, target) + 1, align):
        if n % d == 0:
            best = d
    return best if best is not None else n


def _cparams(*sem):
    return pltpu.CompilerParams(dimension_semantics=sem, vmem_limit_bytes=VMEM_LIMIT)


def _sigmoid(x):
    return 1.0 / (1.0 + jnp.exp(-x))


def _dot(a, b, precision=None):
    return jnp.dot(a, b, preferred_element_type=F32, precision=precision)


def _dot_nt(a, b, precision=None):
    return lax.dot_general(a, b, (((1,), (1,)), ((), ())), preferred_element_type=F32, precision=precision)


def _dot_tn(a, b, precision=None):
    return lax.dot_general(a, b, (((0,), (0,)), ((), ())), preferred_element_type=F32, precision=precision)


def _rmsnorm_kernel(x_ref, g_ref, o_ref):
    x = x_ref[...]
    ms = jnp.mean(x * x, axis=-1, keepdims=True)
    o_ref[...] = (x * lax.rsqrt(ms + EPS_RMS) * g_ref[...]).astype(o_ref.dtype)


def rmsnorm(x, g, out_dtype):
    m, d = x.shape
    bm = _pick(m, 320, 16)
    return pl.pallas_call(
        _rmsnorm_kernel,
        grid=(m // bm,),
        in_specs=[pl.BlockSpec((bm, d), lambda i: (i, 0)),
                  pl.BlockSpec((1, d), lambda i: (0, 0))],
        out_specs=pl.BlockSpec((bm, d), lambda i: (i, 0)),
        out_shape=jax.ShapeDtypeStruct((m, d), out_dtype),
        compiler_params=_cparams("parallel"),
        name="rmsnorm",
    )(x, g.reshape(1, d))


def _mm_kernel(x_ref, w_ref, o_ref):
    o_ref[...] = _dot(x_ref[...], w_ref[...]).astype(o_ref.dtype)


def matmul(x, w, out_dtype, bm_target=1040, bn_target=512):
    m, k = x.shape
    n = w.shape[1]
    bm = _pick(m, bm_target, 16)
    bn = _pick(n, bn_target, LANES)
    return pl.pallas_call(
        _mm_kernel,
        grid=(m // bm, n // bn),
        in_specs=[pl.BlockSpec((bm, k), lambda i, j: (i, 0)),
                  pl.BlockSpec((k, bn), lambda i, j: (0, j))],
        out_specs=pl.BlockSpec((bm, bn), lambda i, j: (i, j)),
        out_shape=jax.ShapeDtypeStruct((m, n), out_dtype),
        compiler_params=_cparams("parallel", "arbitrary"),
        name="matmul",
    )(x, w)


def _mm2_res_kernel(x1_ref, x2_ref, w_ref, r_ref, o_ref, *, k1):
    acc = _dot(x1_ref[...], w_ref[0:k1, :])
    acc = acc + _dot(x2_ref[...], w_ref[k1:, :])
    o_ref[...] = r_ref[...] + acc


def matmul2_residual(x1, x2, w, res, bm_target=1040, bn_target=512):
    m, k1 = x1.shape
    k2 = x2.shape[1]
    n = w.shape[1]
    bm = _pick(m, bm_target, 16)
    bn = _pick(n, bn_target, LANES)
    return pl.pallas_call(
        functools.partial(_mm2_res_kernel, k1=k1),
        grid=(m // bm, n // bn),
        in_specs=[pl.BlockSpec((bm, k1), lambda i, j: (i, 0)),
                  pl.BlockSpec((bm, k2), lambda i, j: (i, 0)),
                  pl.BlockSpec((k1 + k2, bn), lambda i, j: (0, j)),
                  pl.BlockSpec((bm, bn), lambda i, j: (i, j))],
        out_specs=pl.BlockSpec((bm, bn), lambda i, j: (i, j)),
        out_shape=jax.ShapeDtypeStruct((m, n), F32),
        compiler_params=_cparams("parallel", "arbitrary"),
        name="out_proj",
    )(x1, x2, w, res)


def _gateup_kernel(x_ref, wg_ref, wu_ref, o_ref):
    x = x_ref[...]
    g = _dot(x, wg_ref[...])
    u = _dot(x, wu_ref[...])
    o_ref[...] = (g * _sigmoid(g) * u).astype(o_ref.dtype)


def gate_up(x, wg, wu, bm_target=1040, bn_target=256):
    m, k = x.shape
    n = wg.shape[1]
    bm = _pick(m, bm_target, 16)
    bn = _pick(n, bn_target, LANES)
    return pl.pallas_call(
        _gateup_kernel,
        grid=(m // bm, n // bn),
        in_specs=[pl.BlockSpec((bm, k), lambda i, j: (i, 0)),
                  pl.BlockSpec((k, bn), lambda i, j: (0, j)),
                  pl.BlockSpec((k, bn), lambda i, j: (0, j))],
        out_specs=pl.BlockSpec((bm, bn), lambda i, j: (i, j)),
        out_shape=jax.ShapeDtypeStruct((m, n), BF16),
        compiler_params=_cparams("parallel", "arbitrary"),
        name="ffn_gate_up",
    )(x, wg, wu)


def _mm_res_kernel(x_ref, w_ref, r_ref, o_ref):
    o_ref[...] = r_ref[...] + _dot(x_ref[...], w_ref[...])


def matmul_residual(x, w, res, bm_target=520, bn_target=256):
    m, k = x.shape
    n = w.shape[1]
    bm = _pick(m, bm_target, 16)
    bn = _pick(n, bn_target, LANES)
    return pl.pallas_call(
        _mm_res_kernel,
        grid=(m // bm, n // bn),
        in_specs=[pl.BlockSpec((bm, k), lambda i, j: (i, 0)),
                  pl.BlockSpec((k, bn), lambda i, j: (0, j)),
                  pl.BlockSpec((bm, bn), lambda i, j: (i, j))],
        out_specs=pl.BlockSpec((bm, bn), lambda i, j: (i, j)),
        out_shape=jax.ShapeDtypeStruct((m, n), F32),
        compiler_params=_cparams("parallel", "arbitrary"),
        name="ffn_down",
    )(x, w, res)


def _head_sum(x, bd):
    cols = x.shape[1] // LANES
    parts = [_dot(x[:, c * LANES:(c + 1) * LANES], bd, HI) for c in range(cols)]
    return parts[0] if cols == 1 else jnp.concatenate(parts, axis=1)


def _rwkv_prep_kernel(*refs, dr, rw, ra, rg, has_vgate):
    it = iter(refs)
    p_ref, prev_ref, mu_ref = next(it), next(it), next(it)
    w0_ref, w2_ref, a0_ref, a2_ref, g2_ref = next(it), next(it), next(it), next(it), next(it)
    kk_ref, ka_ref, rk_ref, bd_ref = next(it), next(it), next(it), next(it)
    if has_vgate:
        vf_ref, v0_ref, v1_ref, v2_ref = next(it), next(it), next(it), next(it)
    r_o, lw_o, k_o, v_o, kk_o, be_o, g_o, bo_o = (next(it) for _ in range(8))

    p = p_ref[...]
    xs = p + (prev_ref[...] - p) * mu_ref[...]
    xr = xs[:, 0:dr]
    xk = xs[:, dr:2 * dr]
    xv = xs[:, 2 * dr:3 * dr]
    o = 3 * dr
    xw = xs[:, o:o + rw]
    xa = xs[:, o + rw:o + rw + ra]
    xg = xs[:, o + rw + ra:o + rw + ra + rg]

    d = w0_ref[...] + _dot(jnp.tanh(xw), w2_ref[...], HI)
    nd = -d
    softplus = jnp.maximum(nd, 0.0) + jnp.log(1.0 + jnp.exp(-jnp.abs(nd)))
    lw = -jnp.exp(-softplus - 0.5)
    a = _sigmoid(a0_ref[...] + _dot(xa, a2_ref[...], HI))
    g = _dot(_sigmoid(xg), g2_ref[...], HI)
    if has_vgate:
        vg = _sigmoid(v0_ref[...] + _dot(_dot(xv, v1_ref[...], HI), v2_ref[...], HI))
        v = xv + (vf_ref[...] - xv) * vg
    else:
        v = xv
    bd = bd_ref[...]
    kk = xk * kk_ref[...]
    kk = kk / jnp.maximum(jnp.sqrt(_head_sum(kk * kk, bd)), 1e-12)
    k = xk * (1.0 + (a - 1.0) * ka_ref[...])
    bonus = _head_sum(xr * k * rk_ref[...], bd) * v

    r_o[...] = xr
    lw_o[...] = lw
    k_o[...] = k
    v_o[...] = v
    kk_o[...] = kk
    be_o[...] = kk * a
    g_o[...] = g
    bo_o[...] = bonus


def rwkv_prep(proj, prev, mu, w0, w2, a0, a2, g2, k_k, k_a, r_k, bd, vgate, dr, rw, ra, rg):
    m = proj.shape[0]
    rpp = 3 * dr + rw + ra + rg
    tr = _pick(m, 64, 8)
    row = lambda i: (i, 0)
    fix = lambda i: (0, 0)
    vec = lambda x: x.reshape(1, -1)
    args = [proj, prev, vec(mu), vec(w0), w2, vec(a0), a2, g2, vec(k_k), vec(k_a), vec(r_k), bd]
    specs = [pl.BlockSpec((tr, rpp), row), pl.BlockSpec((tr, rpp), row), pl.BlockSpec((1, rpp), fix),
             pl.BlockSpec((1, dr), fix), pl.BlockSpec(w2.shape, fix), pl.BlockSpec((1, dr), fix),
             pl.BlockSpec(a2.shape, fix), pl.BlockSpec(g2.shape, fix), pl.BlockSpec((1, dr), fix),
             pl.BlockSpec((1, dr), fix), pl.BlockSpec((1, dr), fix), pl.BlockSpec((LANES, LANES), fix)]
    if vgate is not None:
        v_first, v0, v1, v2 = vgate
        args += [v_first, vec(v0), v1, v2]
        specs += [pl.BlockSpec((tr, dr), row), pl.BlockSpec((1, dr), fix),
                  pl.BlockSpec(v1.shape, fix), pl.BlockSpec(v2.shape, fix)]
    out = jax.ShapeDtypeStruct((m, dr), F32)
    return pl.pallas_call(
        functools.partial(_rwkv_prep_kernel, dr=dr, rw=rw, ra=ra, rg=rg, has_vgate=vgate is not None),
        grid=(m // tr,),
        in_specs=specs,
        out_specs=[pl.BlockSpec((tr, dr), row)] * 8,
        out_shape=[out] * 8,
        compiler_params=_cparams("parallel"),
        name="rwkv_prep",
    )(*args)


def _wkv_chunk_kernel(r_ref, lw_ref, k_ref, v_ref, kk_ref, be_ref, s0_ref, y_ref, so_ref, bd_ref, *, nh):
    c = pl.program_id(2)
    C = CHUNK

    @pl.when(c == 0)
    def _():
        bd_ref[...] = s0_ref[0, 0]

    r = r_ref[...]
    lw = lw_ref[...]
    k = k_ref[...]
    v = v_ref[...]
    kk = kk_ref[...]
    be = be_ref[...]
    bd = bd_ref[...]

    row = lax.broadcasted_iota(jnp.int32, (C, C), 0)
    col = lax.broadcasted_iota(jnp.int32, (C, C), 1)
    ltri = (row >= col).astype(F32)
    cum = _dot(ltri, lw, HI)
    cl = cum[C - 1:C, :]
    wd = jnp.exp(cum)
    wi = jnp.exp(-cum)
    we = jnp.exp(cl - cum)
    rb = r * wd
    kb = k * wi
    ab = -kk * jnp.exp(cum - lw)
    bb = be * wi

    lane = lax.broadcasted_iota(jnp.int32, (1, 2 * nh), 1)
    m_a = (lane < nh).astype(F32)
    m_b = 1.0 - m_a

    lhs = jnp.concatenate([ab * m_a, ab * m_b, rb * m_a, rb * m_b], axis=0)
    rhs = jnp.concatenate([bb, kb, bd], axis=0)
    p = _dot_nt(lhs, rhs, HI)
    a_s = p[0:C, 2 * C:] + p[C:2 * C, 2 * C:]
    r_s = p[2 * C:3 * C, 2 * C:] + p[3 * C:4 * C, 2 * C:]

    trow = lax.broadcasted_iota(jnp.int32, (C, 2 * C), 0)
    tcol = lax.broadcasted_iota(jnp.int32, (C, 2 * C), 1)
    tcol = jnp.where(tcol >= C, tcol - C, tcol)
    strict = trow > tcol
    incl = trow >= tcol
    first = lax.broadcasted_iota(jnp.int32, (C, 2 * C), 1) < C
    pa = jnp.where(strict, p[0:C, 0:2 * C], 0.0)
    pb = jnp.where(strict, p[C:2 * C, 0:2 * C], 0.0)
    pra = jnp.where(incl, p[2 * C:3 * C, 0:2 * C], 0.0)
    prb = jnp.where(incl, p[3 * C:4 * C, 0:2 * C], 0.0)

    v_a = v * m_a
    v_b = v * m_b
    ak = jnp.concatenate([jnp.where(first, 0.0, pa), jnp.where(first, 0.0, pb)], axis=1)
    z = a_s + _dot(ak, jnp.concatenate([v_a, v_a, v_b, v_b], axis=0), HI)

    mc = jnp.where(first, pa, pltpu.roll(pb, C, axis=1))
    x = z
    steps = C.bit_length() - 1
    for s in range(steps):
        xa = x * m_a
        xb = x * m_b
        if s < steps - 1:
            top = jnp.concatenate([jnp.where(first, mc, 0.0), xa], axis=1)
            bot = jnp.concatenate([jnp.where(first, 0.0, mc), xb], axis=1)
            out = _dot(mc, jnp.concatenate([top, bot], axis=0), HI)
            mc = out[:, 0:2 * C]
            x = x + out[:, 2 * C:]
        else:
            x = x + _dot(mc, jnp.concatenate([xa, xb], axis=0), HI)
    u = x
    u_a = u * m_a
    u_b = u * m_b

    y = r_s + _dot(jnp.concatenate([pra, prb], axis=1),
                   jnp.concatenate([u_a, v_a, u_b, v_b], axis=0), HI)
    y_ref[...] = y

    brow = lax.broadcasted_iota(jnp.int32, (2 * nh, 2 * nh), 0) < nh
    bcol = lax.broadcasted_iota(jnp.int32, (2 * nh, 2 * nh), 1) < nh
    upd = _dot_tn(jnp.concatenate([u, v], axis=0), jnp.concatenate([be * we, k * we], axis=0), HI)
    bd_new = bd * jnp.exp(cl) + jnp.where(brow == bcol, upd, 0.0)
    bd_ref[...] = bd_new

    @pl.when(c == pl.num_programs(2) - 1)
    def _():
        so_ref[0, 0] = bd_new


def wkv_chunk(r, lw, k, v, kk, be, s0_bd, batch, seq, nh):
    dr = r.shape[1]
    pairs = dr // (2 * nh)
    nc = seq // CHUNK
    tile = pl.BlockSpec((CHUNK, 2 * nh), lambda b, h, c: (b * nc + c, h))
    st = pl.BlockSpec((1, 1, 2 * nh, 2 * nh), lambda b, h, c: (b, h, 0, 0))
    return pl.pallas_call(
        functools.partial(_wkv_chunk_kernel, nh=nh),
        grid=(batch, pairs, nc),
        in_specs=[tile] * 6 + [st],
        out_specs=[tile, st],
        out_shape=[jax.ShapeDtypeStruct((batch * seq, dr), F32),
                   jax.ShapeDtypeStruct(s0_bd.shape, F32)],
        scratch_shapes=[pltpu.VMEM((2 * nh, 2 * nh), F32)],
        compiler_params=_cparams("parallel", "parallel", "arbitrary"),
        name="wkv7_chunk",
    )(r, lw, k, v, kk, be, s0_bd)


def _wkv_step_kernel(r_ref, lw_ref, k_ref, v_ref, kk_ref, be_ref, s_ref, y_ref, so_ref):
    s = s_ref[...]
    n = s.shape[-1]
    eye = (lax.broadcasted_iota(jnp.int32, (n, n), 0) == lax.broadcasted_iota(jnp.int32, (n, n), 1))[None]
    sa = jnp.sum(s * kk_ref[...], axis=-1, keepdims=True)
    vcol = jnp.sum(jnp.where(eye, v_ref[...], 0.0), axis=-1, keepdims=True)
    s_new = s * jnp.exp(lw_ref[...]) - sa * be_ref[...] + vcol * k_ref[...]
    ycol = jnp.sum(s_new * r_ref[...], axis=-1, keepdims=True)
    y_ref[...] = jnp.sum(jnp.where(eye, ycol, 0.0), axis=-2, keepdims=True)
    so_ref[...] = s_new


def wkv_step(r, lw, k, v, kk, be, state):
    bh, n, _ = state.shape
    g = _pick(bh, 64, 1)
    vec = pl.BlockSpec((g, 1, n), lambda i: (i, 0, 0))
    st = pl.BlockSpec((g, n, n), lambda i: (i, 0, 0))
    return pl.pallas_call(
        _wkv_step_kernel,
        grid=(bh // g,),
        in_specs=[vec] * 6 + [st],
        out_specs=[vec, st],
        out_shape=[jax.ShapeDtypeStruct((bh, 1, n), F32), jax.ShapeDtypeStruct(state.shape, F32)],
        compiler_params=_cparams("parallel"),
        name="wkv7_step",
    )(r, lw, k, v, kk, be, state)


def _rwkv_post_kernel(y_ref, bo_ref, g_ref, w_ref, b_ref, bd_ref, o_ref, *, nh):
    y = y_ref[...]
    bd = bd_ref[...]
    mean = _head_sum(y, bd) * (1.0 / nh)
    d = y - mean
    var = _head_sum(d * d, bd) * (1.0 / nh)
    yn = d * lax.rsqrt(var + EPS_LNX) * w_ref[...] + b_ref[...]
    o_ref[...] = ((yn + bo_ref[...]) * g_ref[...]).astype(o_ref.dtype)


def rwkv_post(y, bonus, g, lnx_w, lnx_b, bd, nh):
    m, dr = y.shape
    tr = _pick(m, 256, 16)
    row = lambda i: (i, 0)
    fix = lambda i: (0, 0)
    return pl.pallas_call(
        functools.partial(_rwkv_post_kernel, nh=nh),
        grid=(m // tr,),
        in_specs=[pl.BlockSpec((tr, dr), row)] * 3 + [pl.BlockSpec((1, dr), fix)] * 2
                 + [pl.BlockSpec((LANES, LANES), fix)],
        out_specs=pl.BlockSpec((tr, dr), row),
        out_shape=jax.ShapeDtypeStruct((m, dr), BF16),
        compiler_params=_cparams("parallel"),
        name="rwkv_post",
    )(y, bonus, g, lnx_w.reshape(1, dr), lnx_b.reshape(1, dr), bd)


def _hgrn_gates(qh, fh, lb):
    f = lb + (1.0 - lb) * _sigmoid(fh)
    logf = jnp.log(jnp.maximum(f, MIN_FORGET))
    kin = (1.0 - lb) * _sigmoid(-fh)
    q = qh * _sigmoid(qh)
    return q, kin, logf


def _hgrn_out(o, gh, nw):
    o = o * lax.rsqrt(jnp.mean(o * o, axis=-1, keepdims=True) + EPS_HGRN)
    return o * nw * (gh * _sigmoid(gh))


def _hgrn_chunk_kernel(q_ref, f_ref, i_ref, g_ref, lb_ref, nw_ref, s0_ref, o_ref, so_ref, st_ref):
    c = pl.program_id(2)
    C = CHUNK

    @pl.when(c == 0)
    def _():
        st_ref[...] = s0_ref[0, 0].T

    q, kin, logf = _hgrn_gates(q_ref[...], f_ref[...], lb_ref[...])
    v = i_ref[...]
    st = st_ref[...]

    row = lax.broadcasted_iota(jnp.int32, (C, C), 0)
    col = lax.broadcasted_iota(jnp.int32, (C, C), 1)
    b = _dot((row >= col).astype(F32), logf, HI)
    bl = b[C - 1:C, :]
    o_inter = _dot_nt(q * jnp.exp(b), st, HI)

    lane = lax.broadcasted_iota(jnp.int32, (SUB, C), 1)
    trow = lax.broadcasted_iota(jnp.int32, (SUB, 1), 0)
    blocks = []
    for blk in range(C // SUB):
        lo = blk * SUB
        q_i = q[lo:lo + SUB]
        b_i = b[lo:lo + SUB]
        acc = jnp.zeros((SUB, C), F32)
        for s in range(SUB):
            e = jnp.exp(jnp.minimum(b_i - b[lo + s:lo + s + 1], 0.0))
            colsum = jnp.sum(q_i * kin[lo + s:lo + s + 1] * e, axis=-1, keepdims=True)
            colsum = jnp.where(trow >= s, colsum, 0.0)
            acc = jnp.where(lane == lo + s, colsum, acc)
        if blk > 0:
            c_i = b[lo - 1:lo]
            off = _dot_nt(q_i * jnp.exp(b_i - c_i), kin * jnp.exp(jnp.minimum(c_i - b, 0.0)), HI)
            acc = jnp.where(lane < lo, off, acc)
        blocks.append(acc)
    a = jnp.concatenate(blocks, axis=0)
    o = o_inter + _dot(a, v, HI)
    o_ref[...] = _hgrn_out(o, g_ref[...], nw_ref[...]).astype(o_ref.dtype)

    st_new = st * jnp.exp(bl) + _dot_tn(v, kin * jnp.exp(bl - b), HI)
    st_ref[...] = st_new

    @pl.when(c == pl.num_programs(2) - 1)
    def _():
        so_ref[0, 0] = st_new.T


def hgrn_chunk(proj, col0, lb, nw, s0, batch, seq, heads, dh):
    nc = seq // CHUNK
    c0 = col0 // dh

    def seg(j):
        return pl.BlockSpec((CHUNK, dh), lambda b, h, c: (b * nc + c, c0 + j * heads + h))

    vec = pl.BlockSpec((1, dh), lambda b, h, c: (0, h))
    st = pl.BlockSpec((1, 1, dh, dh), lambda b, h, c: (b, h, 0, 0))
    return pl.pallas_call(
        _hgrn_chunk_kernel,
        grid=(batch, heads, nc),
        in_specs=[seg(0), seg(1), seg(2), seg(3), vec, vec, st],
        out_specs=[pl.BlockSpec((CHUNK, dh), lambda b, h, c: (b * nc + c, h)), st],
        out_shape=[jax.ShapeDtypeStruct((batch * seq, heads * dh), BF16),
                   jax.ShapeDtypeStruct(s0.shape, F32)],
        scratch_shapes=[pltpu.VMEM((dh, dh), F32)],
        compiler_params=_cparams("parallel", "parallel", "arbitrary"),
        name="hgrn2_chunk",
    )(proj, proj, proj, proj, lb.reshape(1, -1), nw.reshape(1, -1), s0)


def _hgrn_step_kernel(q_ref, f_ref, i_ref, g_ref, lb_ref, nw_ref, s_ref, o_ref, so_ref):
    s = s_ref[...]
    n = s.shape[-1]
    q, kin, logf = _hgrn_gates(q_ref[...], f_ref[...], lb_ref[...])
    eye = (lax.broadcasted_iota(jnp.int32, (n, n), 0) == lax.broadcasted_iota(jnp.int32, (n, n), 1))[None]

    def col(x):
        return jnp.sum(jnp.where(eye, x, 0.0), axis=-1, keepdims=True)

    s_new = s * col(jnp.exp(logf)) + col(kin) * i_ref[...]
    o = jnp.sum(s_new * col(q), axis=-2, keepdims=True)
    o_ref[...] = _hgrn_out(o, g_ref[...], nw_ref[...]).astype(o_ref.dtype)
    so_ref[...] = s_new


def hgrn_step(qh, fh, ih, gh, lb, nw, state, heads):
    bh, n, _ = state.shape
    g = _pick(heads, 16, 1)
    hb = heads // g
    vec = pl.BlockSpec((g, 1, n), lambda i: (i, 0, 0))
    par = pl.BlockSpec((g, 1, n), lambda i: (i % hb, 0, 0))
    st = pl.BlockSpec((g, n, n), lambda i: (i, 0, 0))
    return pl.pallas_call(
        _hgrn_step_kernel,
        grid=(bh // g,),
        in_specs=[vec] * 4 + [par, par, st],
        out_specs=[vec, st],
        out_shape=[jax.ShapeDtypeStruct((bh, 1, n), BF16), jax.ShapeDtypeStruct(state.shape, F32)],
        compiler_params=_cparams("parallel"),
        name="hgrn2_step",
    )(qh, fh, ih, gh, lb, nw, state)


def _pad_cols(x, segs):
    parts = []
    for lo, hi, wide in segs:
        s = x[..., lo:hi]
        if wide > hi - lo:
            s = jnp.pad(s, [(0, 0)] * (x.ndim - 1) + [(0, wide - (hi - lo))])
        parts.append(s)
    return jnp.concatenate(parts, axis=-1)


def _unpad_cols(x, segs):
    parts, o = [], 0
    for lo, hi, wide in segs:
        parts.append(x[..., o:o + hi - lo])
        o += wide
    return jnp.concatenate(parts, axis=-1)


def _pad_rows(w, rows):
    return jnp.pad(w, ((0, rows - w.shape[0]), (0, 0)))


def kernel(x_prompt, x_sample, state_wkv, state_hgrn, state_shift, norm_mix, w_in, mu_shift, w0, w2, a0, a2, v0, v1, v2, g2, k_k, k_a, r_k, lnx_w, lnx_b, hgrn_lb, hgrn_norm, w_out, norm_ffn, w_gate, w_up, w_down, norm_final):
    batch, seq, dm = x_prompt.shape
    dec = x_sample.shape[0]
    depth = w_in.shape[0]
    _, _, rh, nh, _ = state_wkv.shape
    _, _, hh, dh, _ = state_hgrn.shape
    dr, dhg = rh * nh, hh * dh
    rw_, ra_, rg_ = w2.shape[1], a2.shape[1], g2.shape[1]
    rp = 3 * dr + rw_ + ra_ + rg_
    assert x_sample.shape[1] == 1 and seq % CHUNK == 0 and CHUNK % SUB == 0
    assert 2 * nh == LANES and dh == LANES and rh % 2 == 0 and dr % LANES == 0
    assert w_in.shape[2] == rp + 4 * dhg and state_shift.shape[2] == rp
    rw, ra, rg = (_round_up(t, LANES) for t in (rw_, ra_, rg_))
    rpp = 3 * dr + rw + ra + rg
    segs = [(0, 3 * dr, 3 * dr), (3 * dr, 3 * dr + rw_, rw), (3 * dr + rw_, 3 * dr + rw_ + ra_, ra),
            (3 * dr + rw_ + ra_, rp, rg)]
    mp = batch * seq
    m = mp + dec

    pz = jax.nn.softmax(hgrn_lb.astype(F32), axis=0)
    lbs = jnp.cumsum(pz, axis=0) - pz[0:1]

    lane_head = jnp.arange(LANES) // nh
    bd = (lane_head[:, None] == lane_head[None, :]).astype(F32)

    x = jnp.concatenate([x_prompt.reshape(mp, dm), x_sample.reshape(dec, dm)], axis=0)
    zero_bd = jnp.zeros((batch, rh // 2, LANES, LANES), F32)
    zero_hg = jnp.zeros((batch, hh, dh, dh), F32)

    wkv_p, hgrn_p, shift_p, wkv_s, hgrn_s, shift_s = [], [], [], [], [], []
    v_first = None
    for l in range(depth):
        w_in_l = jnp.concatenate([_pad_cols(w_in[l][:, :rp], segs), w_in[l][:, rp:]], axis=1).astype(BF16)
        h = rmsnorm(x, norm_mix[l], BF16)
        proj = matmul(h, w_in_l, F32)

        pr = proj[:, :rpp]
        pr_p = pr[:mp].reshape(batch, seq, rpp)
        prev = jnp.concatenate([
            jnp.concatenate([jnp.zeros((batch, 1, rpp), F32), pr_p[:, :-1]], axis=1).reshape(mp, rpp),
            _pad_cols(state_shift[l], segs)], axis=0)
        vgate = None if l == 0 else (v_first, v0[l - 1], v1[l - 1], v2[l - 1])
        r, lw, k, v, kk, be, g, bonus = rwkv_prep(
            proj, prev, _pad_cols(mu_shift[l], segs), w0[l], _pad_rows(w2[l], rw), a0[l], _pad_rows(a2[l], ra),
            _pad_rows(g2[l], rg), k_k[l], k_a[l], r_k[l].reshape(-1), bd, vgate, dr, rw, ra, rg)
        if l == 0:
            v_first = v
        y_p, bd_out = wkv_chunk(r, lw, k, v, kk, be, zero_bd, batch, seq, nh)
        rows = lambda t: t[mp:].reshape(dec * rh, 1, nh)
        y_s, wkv_new_s = wkv_step(rows(r), rows(lw), rows(k), rows(v), rows(kk), rows(be),
                                  state_wkv[l].reshape(dec * rh, nh, nh))
        y = jnp.concatenate([y_p, y_s.reshape(dec, dr)], axis=0)
        y_rwkv = rwkv_post(y, bonus, g, lnx_w[l], lnx_b[l], bd, nh)
        wkv_p.append(jnp.stack([bd_out[:, :, :nh, :nh], bd_out[:, :, nh:, nh:]], axis=2).reshape(batch, rh, nh, nh))
        wkv_s.append(wkv_new_s.reshape(dec, rh, nh, nh))
        shift_p.append(_unpad_cols(pr_p[:, -1], segs))
        shift_s.append(_unpad_cols(pr[mp:], segs))

        o_p, hg_out = hgrn_chunk(proj, rpp, lbs[l], hgrn_norm[l], zero_hg, batch, seq, hh, dh)
        ph_s = proj[mp:, rpp:].reshape(dec, 4, hh, dh).transpose(1, 0, 2, 3).reshape(4, dec * hh, 1, dh)
        o_s, hgrn_new_s = hgrn_step(ph_s[0], ph_s[1], ph_s[2], ph_s[3], lbs[l].reshape(hh, 1, dh),
                                    hgrn_norm[l].reshape(hh, 1, dh), state_hgrn[l].reshape(dec * hh, dh, dh), hh)
        y_hgrn = jnp.concatenate([o_p, o_s.reshape(dec, dhg)], axis=0)
        hgrn_p.append(hg_out)
        hgrn_s.append(hgrn_new_s.reshape(dec, hh, dh, dh))

        x = matmul2_residual(y_rwkv, y_hgrn, w_out[l].astype(BF16), x)

        h = rmsnorm(x, norm_ffn[l], BF16)
        act = gate_up(h, w_gate[l].astype(BF16), w_up[l].astype(BF16))
        x = matmul_residual(act, w_down[l].astype(BF16), x)

    y = rmsnorm(x, norm_final, F32)
    return (y[:mp].reshape(batch, seq, dm), y[mp:].reshape(dec, 1, dm),
            jnp.stack(wkv_p), jnp.stack(hgrn_p), jnp.stack(shift_p),
            jnp.stack(wkv_s), jnp.stack(hgrn_s), jnp.stack(shift_s))
```

```python
import functools
import math

import jax
import jax.numpy as jnp
from jax import lax
from jax.experimental import pallas as pl
from jax.experimental.pallas import tpu as pltpu

F32 = jnp.float32
BF16 = jnp.bfloat16
HI = lax.Precision.HIGHEST

LANES = 128
SUBLANES = 8
VMEM_LIMIT = 56 * 1024 * 1024

CHUNK = 64
SUB = 16
EPS_RMS = 1e-6
EPS_LNX = 64e-5
EPS_HGRN = 1e-5
MIN_FORGET = 1e-30


def _round_up(x, m):
    return -(-x // m) * m


def _pick(n, target, align):
    best = None
    for d in range(align, min(n, target) + 1, align):
        if n % d == 0:
            best = d
    return best if best is not None else n


def _cparams(*sem):
    return pltpu.CompilerParams(dimension_semantics=sem, vmem_limit_bytes=VMEM_LIMIT)


def _sigmoid(x):
    return 1.0 / (1.0 + jnp.exp(-x))


def _dot(a, b, precision=None):
    return jnp.dot(a, b, preferred_element_type=F32, precision=precision)


def _dot_nt(a, b, precision=None):
    return lax.dot_general(a, b, (((1,), (1,)), ((), ())), preferred_element_type=F32, precision=precision)


def _dot_tn(a, b, precision=None):
    return lax.dot_general(a, b, (((0,), (0,)), ((), ())), preferred_element_type=F32, precision=precision)


def _split(x):
    hi = x.astype(BF16)
    lo = (x - hi.astype(F32)).astype(BF16)
    return hi, lo


def _l3(x, axis):
    hi, lo = _split(x)
    return jnp.concatenate([hi, hi, lo], axis=axis)


def _r3(x, axis):
    hi, lo = _split(x)
    return jnp.concatenate([hi, lo, hi], axis=axis)


def _dot3(a, b):
    return _dot(_l3(a, 1), _r3(b, 0))


def _dot3_nt(a, b):
    return _dot_nt(_l3(a, 1), _r3(b, 1))


def _dot3_tn(a, b):
    return _dot_tn(_l3(a, 0), _r3(b, 0))


def _cumsum_rows(x):
    c = x.shape[0]
    n = x.shape[1]
    row = lax.broadcasted_iota(jnp.int32, (c, c), 0)
    col = lax.broadcasted_iota(jnp.int32, (c, c), 1)
    ltri = (row >= col).astype(BF16)
    hi = x.astype(BF16)
    r1 = x - hi.astype(F32)
    mid = r1.astype(BF16)
    lo = (r1 - mid.astype(F32)).astype(BF16)
    out = _dot(ltri, jnp.concatenate([hi, mid, lo], axis=1))
    return out[:, 0:n] + out[:, n:2 * n] + out[:, 2 * n:3 * n]


def _rmsnorm_kernel(x_ref, g_ref, o_ref):
    x = x_ref[...]
    ms = jnp.mean(x * x, axis=-1, keepdims=True)
    o_ref[...] = (x * lax.rsqrt(ms + EPS_RMS) * g_ref[...]).astype(o_ref.dtype)


def rmsnorm(x, g, out_dtype, row0=0, rows=None):
    d = x.shape[1]
    rows = x.shape[0] if rows is None else rows
    bm = _pick(math.gcd(rows, row0) if row0 else rows, 320, 16)
    off = row0 // bm
    return pl.pallas_call(
        _rmsnorm_kernel,
        grid=(rows // bm,),
        in_specs=[pl.BlockSpec((bm, d), lambda i: (i + off, 0)),
                  pl.BlockSpec((1, d), lambda i: (0, 0))],
        out_specs=pl.BlockSpec((bm, d), lambda i: (i, 0)),
        out_shape=jax.ShapeDtypeStruct((rows, d), out_dtype),
        compiler_params=_cparams("parallel"),
        name="rmsnorm",
    )(x, g.reshape(1, d))


def _w_in_split_kernel(w_ref, wr_ref, wh_ref, *, rp):
    x = w_ref[...]
    r = x[:, :rp]
    pad = wr_ref.shape[-1] - rp
    if pad:
        r = jnp.concatenate([r, jnp.zeros((r.shape[0], pad), r.dtype)], axis=1)
    wr_ref[...] = r.astype(wr_ref.dtype)
    wh_ref[...] = x[:, rp:].astype(wh_ref.dtype)


def w_in_split(w, rp, rpp):
    depth, rows, n = w.shape
    br = _pick(rows, 128, 16)
    return pl.pallas_call(
        functools.partial(_w_in_split_kernel, rp=rp),
        grid=(depth, rows // br),
        in_specs=[pl.BlockSpec((None, br, n), lambda l, i: (l, i, 0))],
        out_specs=[pl.BlockSpec((None, br, rpp), lambda l, i: (l, i, 0)),
                   pl.BlockSpec((None, br, n - rp), lambda l, i: (l, i, 0))],
        out_shape=[jax.ShapeDtypeStruct((depth, rows, rpp), BF16),
                   jax.ShapeDtypeStruct((depth, rows, n - rp), BF16)],
        compiler_params=_cparams("parallel", "parallel"),
        name="w_in_split",
    )(w)


def _mm_kernel(x_ref, w_ref, o_ref):
    o_ref[...] = _dot(x_ref[...], w_ref[...].astype(BF16)).astype(o_ref.dtype)


def matmul(x, w, layer, out_dtype, bm_target=1040, bn_target=512):
    m, k = x.shape
    n = w.shape[2]
    bm = _pick(m, bm_target, 16)
    bn = _pick(n, bn_target, LANES)
    return pl.pallas_call(
        _mm_kernel,
        grid=(m // bm, n // bn),
        in_specs=[pl.BlockSpec((bm, k), lambda i, j: (i, 0)),
                  pl.BlockSpec((None, k, bn), lambda i, j: (layer, 0, j))],
        out_specs=pl.BlockSpec((bm, bn), lambda i, j: (i, j)),
        out_shape=jax.ShapeDtypeStruct((m, n), out_dtype),
        compiler_params=_cparams("parallel", "arbitrary"),
        name="matmul",
    )(x, w)


def _mm2_res_kernel(x1_ref, x2_ref, w_ref, r_ref, o_ref, *, k1):
    acc = _dot(x1_ref[...], w_ref[0:k1, :].astype(BF16))
    acc = acc + _dot(x2_ref[...], w_ref[k1:, :].astype(BF16))
    o_ref[...] = r_ref[...] + acc


def matmul2_residual(x1, x2, w, layer, res, bm_target=1040, bn_target=512):
    m, k1 = x1.shape
    k2 = x2.shape[1]
    n = w.shape[2]
    bm = _pick(m, bm_target, 16)
    bn = _pick(n, bn_target, LANES)
    return pl.pallas_call(
        functools.partial(_mm2_res_kernel, k1=k1),
        grid=(m // bm, n // bn),
        in_specs=[pl.BlockSpec((bm, k1), lambda i, j: (i, 0)),
                  pl.BlockSpec((bm, k2), lambda i, j: (i, 0)),
                  pl.BlockSpec((None, k1 + k2, bn), lambda i, j: (layer, 0, j)),
                  pl.BlockSpec((bm, bn), lambda i, j: (i, j))],
        out_specs=pl.BlockSpec((bm, bn), lambda i, j: (i, j)),
        out_shape=jax.ShapeDtypeStruct((m, n), F32),
        compiler_params=_cparams("parallel", "arbitrary"),
        name="out_proj",
    )(x1, x2, w, res)


def _gateup_kernel(x_ref, wg_ref, wu_ref, o_ref):
    x = x_ref[...]
    g = _dot(x, wg_ref[...].astype(BF16))
    u = _dot(x, wu_ref[...].astype(BF16))
    o_ref[...] = (g * _sigmoid(g) * u).astype(o_ref.dtype)


def gate_up(x, wg, wu, layer, bm_target=2080, bn_target=256):
    m, k = x.shape
    n = wg.shape[2]
    bm = _pick(m, bm_target, 16)
    bn = _pick(n, bn_target, LANES)
    return pl.pallas_call(
        _gateup_kernel,
        grid=(m // bm, n // bn),
        in_specs=[pl.BlockSpec((bm, k), lambda i, j: (i, 0), pipeline_mode=pl.Buffered(1)),
                  pl.BlockSpec((None, k, bn), lambda i, j: (layer, 0, j)),
                  pl.BlockSpec((None, k, bn), lambda i, j: (layer, 0, j))],
        out_specs=pl.BlockSpec((bm, bn), lambda i, j: (i, j)),
        out_shape=jax.ShapeDtypeStruct((m, n), BF16),
        compiler_params=_cparams("parallel", "arbitrary"),
        name="ffn_gate_up",
    )(x, wg, wu)


def _mm_res_kernel(x_ref, w_ref, r_ref, o_ref):
    o_ref[...] = r_ref[...] + _dot(x_ref[...], w_ref[...].astype(BF16))


def matmul_residual(x, w, layer, res, bm_target=1040, bn_target=256):
    m, k = x.shape
    n = w.shape[2]
    bm = _pick(m, bm_target, 16)
    bn = _pick(n, bn_target, LANES)
    return pl.pallas_call(
        _mm_res_kernel,
        grid=(m // bm, n // bn),
        in_specs=[pl.BlockSpec((bm, k), lambda i, j: (i, 0), pipeline_mode=pl.Buffered(1)),
                  pl.BlockSpec((None, k, bn), lambda i, j: (layer, 0, j)),
                  pl.BlockSpec((bm, bn), lambda i, j: (i, j))],
        out_specs=pl.BlockSpec((bm, bn), lambda i, j: (i, j)),
        out_shape=jax.ShapeDtypeStruct((m, n), F32),
        compiler_params=_cparams("parallel", "arbitrary"),
        name="ffn_down",
    )(x, w, res)


def _head_sum(x, bd2):
    cols = x.shape[1] // LANES
    hi, lo = _split(x)
    parts = [_dot(jnp.concatenate([hi[:, c * LANES:(c + 1) * LANES], lo[:, c * LANES:(c + 1) * LANES]], axis=1), bd2)
             for c in range(cols)]
    return parts[0] if cols == 1 else jnp.concatenate(parts, axis=1)


def _dot3_pre(a, b3):
    return _dot(_l3(a, 1), b3)


def _r3_rows(w):
    return _r3(w.astype(F32), 0)


def _rwkv_prep_kernel(*refs, dr, wins, has_vgate, tr, seq, prompt_tiles):
    it = iter(refs)
    p_ref, halo_ref, s0_ref, ps_ref, mu_ref = next(it), next(it), next(it), next(it), next(it)
    w0_ref, w2_ref, a0_ref, a2_ref, g2_ref = next(it), next(it), next(it), next(it), next(it)
    kk_ref, ka_ref, rk_ref, bd_ref = next(it), next(it), next(it), next(it)
    if has_vgate:
        vf_ref, v0_ref, v1_ref, v2_ref = next(it), next(it), next(it), next(it)
    r_o, lw_o, k_o, v_o, kk_o, be_o, g_o, bo_o = (next(it) for _ in range(8))

    i = pl.program_id(0)
    p = p_ref[...]
    first = jnp.where((i * tr) % seq == 0, s0_ref[0], halo_ref[SUBLANES - 1:SUBLANES, :])
    rowid = lax.broadcasted_iota(jnp.int32, (tr, 1), 0)
    prev = jnp.where(rowid == 0, first, pltpu.roll(p, 1, axis=0))
    prev = jnp.where(i >= prompt_tiles, ps_ref[...], prev)

    xs = p + (prev - p) * mu_ref[...]
    xr = xs[:, 0:dr]
    xk = xs[:, dr:2 * dr]
    xv = xs[:, 2 * dr:3 * dr]
    xw, xa, xg = (xs[:, lo:hi] for lo, hi in wins)

    d = w0_ref[...] + _dot3_pre(jnp.tanh(xw), w2_ref[...])
    nd = -d
    softplus = jnp.maximum(nd, 0.0) + jnp.log(1.0 + jnp.exp(-jnp.abs(nd)))
    lw = -jnp.exp(-softplus - 0.5)
    a = _sigmoid(a0_ref[...] + _dot3_pre(xa, a2_ref[...]))
    g = _dot3_pre(_sigmoid(xg), g2_ref[...])
    if has_vgate:
        vg = _sigmoid(v0_ref[...] + _dot3_pre(_dot3_pre(xv, v1_ref[...]), v2_ref[...]))
        v = xv + (vf_ref[...] - xv) * vg
    else:
        v = xv
    bd = bd_ref[...]
    kk = xk * kk_ref[...]
    kk = kk / jnp.maximum(jnp.sqrt(_head_sum(kk * kk, bd)), 1e-12)
    k = xk * (1.0 + (a - 1.0) * ka_ref[...])
    bonus = _head_sum(xr * k * rk_ref[...], bd) * v

    r_o[...] = xr
    lw_o[...] = lw
    k_o[...] = k
    v_o[...] = v
    kk_o[...] = kk
    be_o[...] = kk * a
    g_o[...] = g
    bo_o[...] = bonus


def rwkv_prep(proj, shift0_p, shift_s, mu, w0, w2, a0, a2, g2, k_k, k_a, r_k, bd, vgate, dims):
    dr, wins, rpp, batch, seq = dims
    m = proj.shape[0]
    dec = shift_s.shape[0]
    mp = batch * seq
    tr = _pick(math.gcd(seq, dec), 128, SUBLANES)
    pt = mp // tr
    row = lambda i: (i, 0)
    vec = lambda x: x.reshape(1, -1)
    const = lambda shape: pl.BlockSpec(shape, lambda i: (0, 0), pipeline_mode=pl.Buffered(1))
    args = [proj, proj, shift0_p, shift_s, vec(mu), vec(w0), w2, vec(a0), a2, g2, vec(k_k), vec(k_a), vec(r_k), bd]
    specs = [pl.BlockSpec((tr, rpp), row),
             pl.BlockSpec((SUBLANES, rpp), lambda i: (jnp.maximum(i * (tr // SUBLANES) - 1, 0), 0)),
             pl.BlockSpec((1, 1, rpp), lambda i: (jnp.minimum((i * tr) // seq, batch - 1), 0, 0)),
             pl.BlockSpec((tr, rpp), lambda i: (jnp.maximum(i - pt, 0), 0)),
             const((1, rpp)),
             const((1, dr)), const(w2.shape), const((1, dr)),
             const(a2.shape), const(g2.shape), const((1, dr)),
             const((1, dr)), const((1, dr)), const((2 * LANES, LANES))]
    if vgate is not None:
        v_first, v0, v1, v2 = vgate
        args += [v_first, vec(v0), v1, v2]
        specs += [pl.BlockSpec((tr, dr), row), const((1, dr)), const(v1.shape), const(v2.shape)]
    out = jax.ShapeDtypeStruct((m, dr), F32)
    return pl.pallas_call(
        functools.partial(_rwkv_prep_kernel, dr=dr, wins=wins, has_vgate=vgate is not None,
                          tr=tr, seq=seq, prompt_tiles=pt),
        grid=(m // tr,),
        in_specs=specs,
        out_specs=[pl.BlockSpec((tr, dr), row)] * 8,
        out_shape=[out] * 8,
        compiler_params=_cparams("parallel"),
        name="rwkv_prep",
    )(*args)


def _wkv_pairs(rb, kb, ab, bb, bh, kh, v, bd, cum, nh):
    C = CHUNK
    G = range(len(rb))
    lane3 = lax.broadcasted_iota(jnp.int32, (1, 6 * nh), 1) % (2 * nh)
    in_a3 = lane3 < nh
    lane = lax.broadcasted_iota(jnp.int32, (1, 2 * nh), 1)
    m_a = (lane < nh).astype(F32)
    m_b = 1.0 - m_a
    zb = jnp.zeros((), BF16)
    trow = lax.broadcasted_iota(jnp.int32, (C, 2 * C), 0)
    tcol = lax.broadcasted_iota(jnp.int32, (C, 2 * C), 1)
    first = tcol < C
    tcol = jnp.where(first, tcol, tcol - C)
    strict = trow > tcol
    incl = trow >= tcol

    def lhs_of(g):
        ab3 = _l3(ab[g], 1)
        rb3 = _l3(rb[g], 1)
        return jnp.concatenate([jnp.where(in_a3, ab3, zb), jnp.where(in_a3, zb, ab3),
                                jnp.where(in_a3, rb3, zb), jnp.where(in_a3, zb, rb3)], axis=0)

    p = [_dot_nt(lhs_of(g), jnp.concatenate([_r3(bb[g], 1), _r3(kb[g], 1), _r3(bd[g], 1)], axis=0)) for g in G]
    a_s = [p[g][0:C, 2 * C:] + p[g][C:2 * C, 2 * C:] for g in G]
    r_s = [p[g][2 * C:3 * C, 2 * C:] + p[g][3 * C:4 * C, 2 * C:] for g in G]
    pa = [jnp.where(strict, p[g][0:C, 0:2 * C], 0.0) for g in G]
    pb = [jnp.where(strict, p[g][C:2 * C, 0:2 * C], 0.0) for g in G]
    pr = [jnp.concatenate([jnp.where(incl, p[g][2 * C:3 * C, 0:2 * C], 0.0),
                           jnp.where(incl, p[g][3 * C:4 * C, 0:2 * C], 0.0)], axis=1) for g in G]

    v_a = [v[g] * m_a for g in G]
    v_b = [v[g] * m_b for g in G]
    ak = [jnp.where(first, pltpu.roll(pa[g], C, axis=1), pb[g]) for g in G]
    z = [_dot3(ak[g], jnp.concatenate([v_a[g], v_b[g]], axis=0)) for g in G]
    x = [a_s[g] + z[g] for g in G]

    mc = [jnp.where(first, pa[g], pltpu.roll(pb[g], C, axis=1)) for g in G]
    steps = C.bit_length() - 1
    for s in range(steps):
        xst = [jnp.concatenate([x[g] * m_a, x[g] * m_b], axis=0) for g in G]
        if s < steps - 1:
            mst = [jnp.concatenate([jnp.where(first, mc[g], 0.0), jnp.where(first, 0.0, mc[g])], axis=0) for g in G]
            out = [_dot3(mc[g], jnp.concatenate([mst[g], xst[g]], axis=1)) for g in G]
            mc = [out[g][:, 0:2 * C] for g in G]
            x = [x[g] + out[g][:, 2 * C:] for g in G]
        else:
            out = [_dot3(mc[g], xst[g]) for g in G]
            x = [x[g] + out[g] for g in G]
    u = x

    yo = [_dot3(pr[g], jnp.concatenate([u[g] * m_a, v_a[g], u[g] * m_b, v_b[g]], axis=0)) for g in G]
    y = [r_s[g] + yo[g] for g in G]

    brow = lax.broadcasted_iota(jnp.int32, (2 * nh, 2 * nh), 0) < nh
    bcol = lax.broadcasted_iota(jnp.int32, (2 * nh, 2 * nh), 1) < nh
    upd = [_dot3_tn(jnp.concatenate([u[g], v[g]], axis=0), jnp.concatenate([bh[g], kh[g]], axis=0)) for g in G]
    bd_new = [bd[g] * jnp.exp(cum[g][C - 1:C, :]) + jnp.where(brow == bcol, upd[g], 0.0) for g in G]
    return y, bd_new


def _wkv_chunk_kernel(r_ref, lw_ref, k_ref, v_ref, kk_ref, be_ref, s0_ref, y_ref, so_ref, bd_ref, *, nh, gp):
    c = pl.program_id(2)
    C = CHUNK
    w = 2 * nh

    @pl.when(c == 0)
    def _():
        bd_ref[...] = s0_ref[0]

    lw = lw_ref[...]
    k = k_ref[...]
    be = be_ref[...]
    cum = _cumsum_rows(lw)
    cl = cum[C - 1:C, :]
    wi = jnp.exp(-cum)
    we = jnp.exp(cl - cum)
    rb = r_ref[...] * jnp.exp(cum)
    kb = k * wi
    ab = -kk_ref[...] * jnp.exp(cum - lw)
    bb = be * wi
    bh = be * we
    kh = k * we
    v = v_ref[...]

    tiles = lambda t: [t[:, g * w:(g + 1) * w] for g in range(gp)]
    ys, bds = _wkv_pairs(tiles(rb), tiles(kb), tiles(ab), tiles(bb), tiles(bh), tiles(kh), tiles(v),
                         [bd_ref[g * w:(g + 1) * w, :] for g in range(gp)], tiles(cum), nh)
    y_ref[...] = ys[0] if gp == 1 else jnp.concatenate(ys, axis=1)
    bd_all = bds[0] if gp == 1 else jnp.concatenate(bds, axis=0)
    bd_ref[...] = bd_all

    @pl.when(c == pl.num_programs(2) - 1)
    def _():
        so_ref[0] = bd_all


def wkv_chunk(r, lw, k, v, kk, be, s0_bd, batch, seq, nh):
    dr = r.shape[1]
    w = 2 * nh
    pairs = dr // w
    gp = _pick(pairs, 8, 1)
    nc = seq // CHUNK
    tile = pl.BlockSpec((CHUNK, gp * w), lambda b, h, c: (b * nc + c, h))
    st = pl.BlockSpec((1, gp * w, w), lambda b, h, c: (b, h, 0))
    return pl.pallas_call(
        functools.partial(_wkv_chunk_kernel, nh=nh, gp=gp),
        grid=(batch, pairs // gp, nc),
        in_specs=[tile] * 6 + [st],
        out_specs=[tile, st],
        out_shape=[jax.ShapeDtypeStruct((batch * seq, dr), F32),
                   jax.ShapeDtypeStruct(s0_bd.shape, F32)],
        scratch_shapes=[pltpu.VMEM((gp * w, w), F32)],
        compiler_params=_cparams("parallel", "parallel", "arbitrary"),
        name="wkv7_chunk",
    )(r, lw, k, v, kk, be, s0_bd)


def _wkv_step_kernel(r_ref, lw_ref, k_ref, v_ref, kk_ref, be_ref, s_ref, *rest):
    *done_refs, y_ref, so_ref = rest
    s = s_ref[...]
    n = s.shape[-1]
    eye = (lax.broadcasted_iota(jnp.int32, (n, n), 0) == lax.broadcasted_iota(jnp.int32, (n, n), 1))[None]
    sa = jnp.sum(s * kk_ref[...], axis=-1, keepdims=True)
    vcol = jnp.sum(jnp.where(eye, v_ref[...], 0.0), axis=-1, keepdims=True)
    s_new = s * jnp.exp(lw_ref[...]) - sa * be_ref[...] + vcol * k_ref[...]
    ycol = jnp.sum(s_new * r_ref[...], axis=-1, keepdims=True)
    y_ref[...] = jnp.sum(jnp.where(eye, ycol, 0.0), axis=-2, keepdims=True)
    _store_stacked(so_ref, done_refs, s_new)


def _store_stacked(so_ref, done_refs, s_new):
    if done_refs:
        for l, d_ref in enumerate(done_refs):
            so_ref[l] = d_ref[...]
        so_ref[len(done_refs)] = s_new
    else:
        so_ref[...] = s_new


def _state_specs(g, n, layer, done):
    st = pl.BlockSpec((g, n, n), lambda i: (i, 0, 0))
    st_in = pl.BlockSpec((None, g, n, n), lambda i: (layer, i, 0, 0))
    if done:
        return st_in, [st] * len(done), pl.BlockSpec((len(done) + 1, g, n, n), lambda i: (0, i, 0, 0))
    return st_in, [], st


def wkv_step(r, lw, k, v, kk, be, state, layer, done=()):
    _, bh, n, _ = state.shape
    g = _pick(bh, 64, 1)
    vec = pl.BlockSpec((g, 1, n), lambda i: (i, 0, 0))
    st_in, done_specs, st_out = _state_specs(g, n, layer, done)
    so_shape = (len(done) + 1, bh, n, n) if done else (bh, n, n)
    return pl.pallas_call(
        _wkv_step_kernel,
        grid=(bh // g,),
        in_specs=[vec] * 6 + [st_in] + done_specs,
        out_specs=[vec, st_out],
        out_shape=[jax.ShapeDtypeStruct((bh, 1, n), F32), jax.ShapeDtypeStruct(so_shape, F32)],
        compiler_params=_cparams("parallel"),
        name="wkv7_step",
    )(r, lw, k, v, kk, be, state, *done)


def _rwkv_post_kernel(yp_ref, ys_ref, bo_ref, g_ref, w_ref, b_ref, bd_ref, o_ref, *, nh, prompt_tiles):
    y = jnp.where(pl.program_id(0) >= prompt_tiles, ys_ref[...], yp_ref[...])
    bd = bd_ref[...]
    mean = _head_sum(y, bd) * (1.0 / nh)
    d = y - mean
    var = _head_sum(d * d, bd) * (1.0 / nh)
    yn = d * lax.rsqrt(var + EPS_LNX) * w_ref[...] + b_ref[...]
    o_ref[...] = ((yn + bo_ref[...]) * g_ref[...]).astype(o_ref.dtype)


def rwkv_post(y_p, y_s, bonus, g, lnx_w, lnx_b, bd, nh):
    mp, dr = y_p.shape
    dec = y_s.shape[0]
    m = mp + dec
    tr = _pick(math.gcd(mp, dec), 256, 16)
    pt = mp // tr
    row = lambda i: (i, 0)
    fix = lambda i: (0, 0)
    return pl.pallas_call(
        functools.partial(_rwkv_post_kernel, nh=nh, prompt_tiles=pt),
        grid=(m // tr,),
        in_specs=[pl.BlockSpec((tr, dr), lambda i: (jnp.minimum(i, pt - 1), 0)),
                  pl.BlockSpec((tr, dr), lambda i: (jnp.maximum(i - pt, 0), 0)),
                  pl.BlockSpec((tr, dr), row), pl.BlockSpec((tr, dr), row),
                  pl.BlockSpec((1, dr), fix), pl.BlockSpec((1, dr), fix),
                  pl.BlockSpec((2 * LANES, LANES), fix)],
        out_specs=pl.BlockSpec((tr, dr), row),
        out_shape=jax.ShapeDtypeStruct((m, dr), BF16),
        compiler_params=_cparams("parallel"),
        name="rwkv_post",
    )(y_p, y_s, bonus, g, lnx_w.reshape(1, dr), lnx_b.reshape(1, dr), bd)


def _hgrn_gates(qh, fh, lb):
    f = lb + (1.0 - lb) * _sigmoid(fh)
    logf = jnp.log(jnp.maximum(f, MIN_FORGET))
    kin = (1.0 - lb) * _sigmoid(-fh)
    q = qh * _sigmoid(qh)
    return q, kin, logf


def _hgrn_out(o, gh, nw):
    o = o * lax.rsqrt(jnp.mean(o * o, axis=-1, keepdims=True) + EPS_HGRN)
    return o * nw * (gh * _sigmoid(gh))


def _hgrn_heads(q, kin, v, b, st):
    C = CHUNK
    H = range(len(q))
    o_inter = [_dot3_nt(q[h] * jnp.exp(b[h]), st[h]) for h in H]

    row = lax.broadcasted_iota(jnp.int32, (C, C), 0)
    col = lax.broadcasted_iota(jnp.int32, (C, C), 1)
    rsub = row % SUB
    scol = lax.broadcasted_iota(jnp.int32, (SUB, C), 1)
    offs = []
    for blk in range(1, C // SUB):
        lo = blk * SUB
        offs.append([_dot3_nt(q[h][lo:lo + SUB] * jnp.exp(b[h][lo:lo + SUB] - b[h][lo - 1:lo]),
                              kin[h] * jnp.exp(jnp.minimum(b[h][lo - 1:lo] - b[h], 0.0))) for h in H])
    a = []
    for h in H:
        diag = jnp.zeros((C, C), F32)
        bk = b[h] - jnp.log(kin[h])
        for dlt in range(SUB):
            if dlt == 0:
                w = q[h] * kin[h]
            else:
                w = q[h] * jnp.exp(jnp.minimum(b[h] - pltpu.roll(bk, dlt, axis=0), 0.0))
            colsum = jnp.sum(w, axis=-1, keepdims=True)
            diag = jnp.where((col == row - dlt) & (rsub >= dlt), colsum, diag)
        blocks = [diag[0:SUB]]
        for blk in range(1, C // SUB):
            lo = blk * SUB
            blocks.append(jnp.where(scol < lo, offs[blk - 1][h], diag[lo:lo + SUB]))
        a.append(jnp.concatenate(blocks, axis=0))
    o_intra = [_dot3(a[h], v[h]) for h in H]
    upd = [_dot3_tn(v[h], kin[h] * jnp.exp(b[h][C - 1:C, :] - b[h])) for h in H]
    o = [o_inter[h] + o_intra[h] for h in H]
    st_new = [st[h] * jnp.exp(b[h][C - 1:C, :]) + upd[h] for h in H]
    return o, st_new


def _hgrn_chunk_kernel(q_ref, f_ref, i_ref, g_ref, lb_ref, nw_ref, s0_ref, o_ref, so_ref, st_ref, *, dh, gh):
    c = pl.program_id(2)

    @pl.when(c == 0)
    def _():
        for h in range(gh):
            st_ref[h * dh:(h + 1) * dh, :] = s0_ref[0, h].T

    q, kin, logf = _hgrn_gates(q_ref[...], f_ref[...], lb_ref[...])
    v = i_ref[...]
    b = _cumsum_rows(logf)
    tiles = lambda t: [t[:, h * dh:(h + 1) * dh] for h in range(gh)]
    os, sts = _hgrn_heads(tiles(q), tiles(kin), tiles(v), tiles(b),
                          [st_ref[h * dh:(h + 1) * dh, :] for h in range(gh)])
    os = [o * lax.rsqrt(jnp.mean(o * o, axis=-1, keepdims=True) + EPS_HGRN) for o in os]
    o = os[0] if gh == 1 else jnp.concatenate(os, axis=1)
    gate = g_ref[...]
    o_ref[...] = (o * nw_ref[...] * (gate * _sigmoid(gate))).astype(o_ref.dtype)
    st_ref[...] = sts[0] if gh == 1 else jnp.concatenate(sts, axis=0)

    @pl.when(c == pl.num_programs(2) - 1)
    def _():
        for h in range(gh):
            so_ref[0, h] = sts[h].T


def hgrn_chunk(proj, lb, nw, s0, batch, seq, heads, dh):
    nc = seq // CHUNK
    gh = _pick(heads, 4, 1)
    wide = gh * dh
    hb = heads // gh

    def seg(j):
        return pl.BlockSpec((CHUNK, wide), lambda b, h, c: (b * nc + c, j * hb + h))

    vec = pl.BlockSpec((1, wide), lambda b, h, c: (0, h))
    st = pl.BlockSpec((1, gh, dh, dh), lambda b, h, c: (b, h, 0, 0))
    return pl.pallas_call(
        functools.partial(_hgrn_chunk_kernel, dh=dh, gh=gh),
        grid=(batch, hb, nc),
        in_specs=[seg(0), seg(1), seg(2), seg(3), vec, vec, st],
        out_specs=[pl.BlockSpec((CHUNK, wide), lambda b, h, c: (b * nc + c, h)), st],
        out_shape=[jax.ShapeDtypeStruct((batch * seq, heads * dh), BF16),
                   jax.ShapeDtypeStruct(s0.shape, F32)],
        scratch_shapes=[pltpu.VMEM((wide, dh), F32)],
        compiler_params=_cparams("parallel", "parallel", "arbitrary"),
        name="hgrn2_chunk",
    )(proj, proj, proj, proj, lb.reshape(1, -1), nw.reshape(1, -1), s0)


def _hgrn_step_kernel(q_ref, f_ref, i_ref, g_ref, lb_ref, nw_ref, s_ref, *rest):
    *done_refs, o_ref, so_ref = rest
    s = s_ref[...]
    n = s.shape[-1]
    q, kin, logf = _hgrn_gates(q_ref[...], f_ref[...], lb_ref[...])
    eye = (lax.broadcasted_iota(jnp.int32, (n, n), 0) == lax.broadcasted_iota(jnp.int32, (n, n), 1))[None]

    def col(x):
        return jnp.sum(jnp.where(eye, x, 0.0), axis=-1, keepdims=True)

    s_new = s * col(jnp.exp(logf)) + col(kin) * i_ref[...]
    o = jnp.sum(s_new * col(q), axis=-2, keepdims=True)
    o_ref[...] = _hgrn_out(o, g_ref[...], nw_ref[...]).astype(o_ref.dtype)
    _store_stacked(so_ref, done_refs, s_new)


def hgrn_step(qh, fh, ih, gh, lb, nw, state, layer, heads, done=()):
    _, bh, n, _ = state.shape
    g = _pick(heads, 16, 1)
    hb = heads // g
    vec = pl.BlockSpec((g, 1, n), lambda i: (i, 0, 0))
    par = pl.BlockSpec((g, 1, n), lambda i: (i % hb, 0, 0))
    st_in, done_specs, st_out = _state_specs(g, n, layer, done)
    so_shape = (len(done) + 1, bh, n, n) if done else (bh, n, n)
    return pl.pallas_call(
        _hgrn_step_kernel,
        grid=(bh // g,),
        in_specs=[vec] * 4 + [par, par, st_in] + done_specs,
        out_specs=[vec, st_out],
        out_shape=[jax.ShapeDtypeStruct((bh, 1, n), BF16), jax.ShapeDtypeStruct(so_shape, F32)],
        compiler_params=_cparams("parallel"),
        name="hgrn2_step",
    )(qh, fh, ih, gh, lb, nw, state, *done)


def _window(lo, hi):
    return (lo // LANES) * LANES, _round_up(hi, LANES)


def _window_rows(w, lo, hi):
    ws, we = _window(lo, hi)
    return jnp.pad(w, ((lo - ws, we - hi), (0, 0)))


def kernel(x_prompt, x_sample, state_wkv, state_hgrn, state_shift, norm_mix, w_in, mu_shift, w0, w2, a0, a2, v0, v1, v2, g2, k_k, k_a, r_k, lnx_w, lnx_b, hgrn_lb, hgrn_norm, w_out, norm_ffn, w_gate, w_up, w_down, norm_final):
    batch, seq, dm = x_prompt.shape
    dec = x_sample.shape[0]
    depth = w_in.shape[0]
    _, _, rh, nh, _ = state_wkv.shape
    _, _, hh, dh, _ = state_hgrn.shape
    dr, dhg = rh * nh, hh * dh
    rw, ra, rg = w2.shape[1], a2.shape[1], g2.shape[1]
    rp = 3 * dr + rw + ra + rg
    n_in = rp + 4 * dhg
    assert x_sample.shape[1] == 1 and seq % CHUNK == 0 and CHUNK % SUB == 0
    assert 2 * nh == LANES and dh == LANES and rh % 2 == 0 and dr % LANES == 0
    assert w_in.shape[2] == n_in and state_shift.shape[2] == rp
    seg_w = (3 * dr, 3 * dr + rw)
    seg_a = (seg_w[1], seg_w[1] + ra)
    seg_g = (seg_a[1], rp)
    rpp = _round_up(rp, LANES)
    mp = batch * seq
    dims = (dr, (_window(*seg_w), _window(*seg_a), _window(*seg_g)), rpp, batch, seq)
    pad_rp = lambda t: jnp.pad(t, [(0, 0)] * (t.ndim - 1) + [(0, rpp - rp)])

    pz = jax.nn.softmax(hgrn_lb.astype(F32), axis=0)
    lbs = jnp.cumsum(pz, axis=0) - pz[0:1]

    lane_head = jnp.arange(LANES) // nh
    bd = (lane_head[:, None] == lane_head[None, :]).astype(BF16)
    bd = jnp.concatenate([bd, bd], axis=0)
    w_r_bf, w_h_bf = w_in_split(w_in, rp, rpp)
    w_down_bf = w_down.astype(BF16)
    state_wkv_f = state_wkv.reshape(depth, dec * rh, nh, nh)
    state_hgrn_f = state_hgrn.reshape(depth, dec * hh, dh, dh)

    x = jnp.concatenate([x_prompt.reshape(mp, dm), x_sample.reshape(dec, dm)], axis=0)
    zero_bd = jnp.zeros((batch, (rh // 2) * LANES, LANES), F32)
    zero_hg = jnp.zeros((batch, hh, dh, dh), F32)
    zero_shift = jnp.zeros((batch, 1, rpp), F32)

    wkv_p, hgrn_p, shift_p, wkv_s, hgrn_s, shift_s = [], [], [], [], [], []
    v_first = None
    for l in range(depth):
        h = rmsnorm(x, norm_mix[l], BF16)
        proj = matmul(h, w_r_bf, l, F32)
        proj_h = matmul(h, w_h_bf, l, F32)

        vgate = None if l == 0 else (v_first, v0[l - 1], _r3_rows(v1[l - 1]), _r3_rows(v2[l - 1]))
        r, lw, k, v, kk, be, g, bonus = rwkv_prep(
            proj, zero_shift, pad_rp(state_shift[l]), pad_rp(mu_shift[l]), w0[l],
            _r3_rows(_window_rows(w2[l], *seg_w)), a0[l], _r3_rows(_window_rows(a2[l], *seg_a)),
            _r3_rows(_window_rows(g2[l], *seg_g)), k_k[l], k_a[l], r_k[l].reshape(-1), bd, vgate, dims)
        if l == 0:
            v_first = v
        y_p, bd_out = wkv_chunk(r, lw, k, v, kk, be, zero_bd, batch, seq, nh)
        rows = lambda t: t[mp:].reshape(dec * rh, 1, nh)
        y_s, wkv_new_s = wkv_step(rows(r), rows(lw), rows(k), rows(v), rows(kk), rows(be),
                                  state_wkv_f, l, wkv_s if l == depth - 1 else ())
        y_rwkv = rwkv_post(y_p, y_s.reshape(dec, dr), bonus, g, lnx_w[l], lnx_b[l], bd, nh)
        bd4 = bd_out.reshape(batch, rh // 2, LANES, LANES)
        wkv_p.append(jnp.stack([bd4[:, :, :nh, :nh], bd4[:, :, nh:, nh:]], axis=2).reshape(batch, rh, nh, nh))
        wkv_s.append(wkv_new_s)
        shift_p.append(jnp.concatenate([proj[(b + 1) * seq - 1:(b + 1) * seq, :rp] for b in range(batch)], axis=0))
        shift_s.append(proj[mp:, :rp])

        o_p, hg_out = hgrn_chunk(proj_h, lbs[l], hgrn_norm[l], zero_hg, batch, seq, hh, dh)
        ph_s = proj_h[mp:].reshape(dec, 4, hh, dh).transpose(1, 0, 2, 3).reshape(4, dec * hh, 1, dh)
        o_s, hgrn_new_s = hgrn_step(ph_s[0], ph_s[1], ph_s[2], ph_s[3], lbs[l].reshape(hh, 1, dh),
                                    hgrn_norm[l].reshape(hh, 1, dh), state_hgrn_f, l, hh,
                                    hgrn_s if l == depth - 1 else ())
        y_hgrn = jnp.concatenate([o_p, o_s.reshape(dec, dhg)], axis=0)
        hgrn_p.append(hg_out)
        hgrn_s.append(hgrn_new_s)

        x = matmul2_residual(y_rwkv, y_hgrn, w_out, l, x)

        h = rmsnorm(x, norm_ffn[l], BF16)
        act = gate_up(h, w_gate, w_up, l)
        x = matmul_residual(act, w_down_bf, l, x)

    y_p = rmsnorm(x, norm_final, F32, 0, mp)
    y_s = rmsnorm(x, norm_final, F32, mp, dec)
    stacked = lambda t, shape: (t[-1] if depth > 1 else t[0][None]).reshape((depth,) + shape)
    return (y_p.reshape(batch, seq, dm), y_s.reshape(dec, 1, dm),
            jnp.stack(wkv_p), jnp.stack(hgrn_p), jnp.stack(shift_p),
            stacked(wkv_s, (dec, rh, nh, nh)), stacked(hgrn_s, (dec, hh, dh, dh)), jnp.stack(shift_s))
```

```python
import functools
import math

import jax
import jax.numpy as jnp
from jax import lax
from jax.experimental import pallas as pl
from jax.experimental.pallas import tpu as pltpu

F32 = jnp.float32
BF16 = jnp.bfloat16
HI = lax.Precision.HIGHEST

LANES = 128
SUBLANES = 8
VMEM_LIMIT = 56 * 1024 * 1024

CHUNK = 64
SUB = 16
EPS_RMS = 1e-6
EPS_LNX = 64e-5
EPS_HGRN = 1e-5
MIN_FORGET = 1e-30


def _round_up(x, m):
    return -(-x // m) * m


def _pick(n, target, align):
    best = None
    for d in range(align, min(n, target) + 1, align):
        if n % d == 0:
            best = d
    return best if best is not None else n


def _cparams(*sem):
    return pltpu.CompilerParams(dimension_semantics=sem, vmem_limit_bytes=VMEM_LIMIT)


def _sigmoid(x):
    return 1.0 / (1.0 + jnp.exp(-x))


def _dot(a, b, precision=None):
    return jnp.dot(a, b, preferred_element_type=F32, precision=precision)


def _dot_nt(a, b, precision=None):
    return lax.dot_general(a, b, (((1,), (1,)), ((), ())), preferred_element_type=F32, precision=precision)


def _dot_tn(a, b, precision=None):
    return lax.dot_general(a, b, (((0,), (0,)), ((), ())), preferred_element_type=F32, precision=precision)


def _split(x):
    hi = x.astype(BF16)
    lo = (x - hi.astype(F32)).astype(BF16)
    return hi, lo


def _l3(x, axis):
    hi, lo = _split(x)
    return jnp.concatenate([hi, hi, lo], axis=axis)


def _r3(x, axis):
    hi, lo = _split(x)
    return jnp.concatenate([hi, lo, hi], axis=axis)


def _dot3(a, b):
    return _dot(_l3(a, 1), _r3(b, 0))


def _dot3_nt(a, b):
    return _dot_nt(_l3(a, 1), _r3(b, 1))


def _dot3_tn(a, b):
    return _dot_tn(_l3(a, 0), _r3(b, 0))


def _cumsum_rows(x):
    c = x.shape[0]
    n = x.shape[1]
    row = lax.broadcasted_iota(jnp.int32, (c, c), 0)
    col = lax.broadcasted_iota(jnp.int32, (c, c), 1)
    ltri = (row >= col).astype(BF16)
    hi = x.astype(BF16)
    r1 = x - hi.astype(F32)
    mid = r1.astype(BF16)
    lo = (r1 - mid.astype(F32)).astype(BF16)
    out = _dot(ltri, jnp.concatenate([hi, mid, lo], axis=1))
    return out[:, 0:n] + out[:, n:2 * n] + out[:, 2 * n:3 * n]


def _rmsnorm_kernel(x_ref, g_ref, o_ref):
    x = x_ref[...]
    ms = jnp.mean(x * x, axis=-1, keepdims=True)
    o_ref[...] = (x * lax.rsqrt(ms + EPS_RMS) * g_ref[...]).astype(o_ref.dtype)


def rmsnorm(x, g, out_dtype, row0=0, rows=None):
    d = x.shape[1]
    rows = x.shape[0] if rows is None else rows
    bm = _pick(math.gcd(rows, row0) if row0 else rows, 320, 16)
    off = row0 // bm
    return pl.pallas_call(
        _rmsnorm_kernel,
        grid=(rows // bm,),
        in_specs=[pl.BlockSpec((bm, d), lambda i: (i + off, 0)),
                  pl.BlockSpec((1, d), lambda i: (0, 0))],
        out_specs=pl.BlockSpec((bm, d), lambda i: (i, 0)),
        out_shape=jax.ShapeDtypeStruct((rows, d), out_dtype),
        compiler_params=_cparams("parallel"),
        name="rmsnorm",
    )(x, g.reshape(1, d))


def _mm_nt_kernel(x_ref, wt_ref, o_ref):
    o_ref[...] = _dot_nt(x_ref[...], wt_ref[...]).astype(o_ref.dtype)


def matmul_nt(x, wt, layer, out_dtype, bm_target=1040, bn_target=512):
    m, k = x.shape
    n = wt.shape[1]
    bm = _pick(m, bm_target, 16)
    bn = _pick(n, bn_target, LANES)
    return pl.pallas_call(
        _mm_nt_kernel,
        grid=(m // bm, n // bn),
        in_specs=[pl.BlockSpec((bm, k), lambda i, j: (i, 0)),
                  pl.BlockSpec((None, bn, k), lambda i, j: (layer, j, 0))],
        out_specs=pl.BlockSpec((bm, bn), lambda i, j: (i, j)),
        out_shape=jax.ShapeDtypeStruct((m, n), out_dtype),
        compiler_params=_cparams("parallel", "arbitrary"),
        name="matmul_nt",
    )(x, wt)


def _mm2_res_kernel(x1_ref, x2_ref, w_ref, r_ref, o_ref, *, k1):
    acc = _dot(x1_ref[...], w_ref[0:k1, :].astype(BF16))
    acc = acc + _dot(x2_ref[...], w_ref[k1:, :].astype(BF16))
    o_ref[...] = r_ref[...] + acc


def matmul2_residual(x1, x2, w, layer, res, bm_target=1040, bn_target=512):
    m, k1 = x1.shape
    k2 = x2.shape[1]
    n = w.shape[2]
    bm = _pick(m, bm_target, 16)
    bn = _pick(n, bn_target, LANES)
    return pl.pallas_call(
        functools.partial(_mm2_res_kernel, k1=k1),
        grid=(m // bm, n // bn),
        in_specs=[pl.BlockSpec((bm, k1), lambda i, j: (i, 0)),
                  pl.BlockSpec((bm, k2), lambda i, j: (i, 0)),
                  pl.BlockSpec((None, k1 + k2, bn), lambda i, j: (layer, 0, j)),
                  pl.BlockSpec((bm, bn), lambda i, j: (i, j))],
        out_specs=pl.BlockSpec((bm, bn), lambda i, j: (i, j)),
        out_shape=jax.ShapeDtypeStruct((m, n), F32),
        compiler_params=_cparams("parallel", "arbitrary"),
        name="out_proj",
    )(x1, x2, w, res)


def _gateup_kernel(x_ref, wg_ref, wu_ref, o_ref):
    x = x_ref[...]
    g = _dot(x, wg_ref[...].astype(BF16))
    u = _dot(x, wu_ref[...].astype(BF16))
    o_ref[...] = (g * _sigmoid(g) * u).astype(o_ref.dtype)


def gate_up(x, wg, wu, layer, bm_target=2080, bn_target=256):
    m, k = x.shape
    n = wg.shape[2]
    bm = _pick(m, bm_target, 16)
    bn = _pick(n, bn_target, LANES)
    return pl.pallas_call(
        _gateup_kernel,
        grid=(m // bm, n // bn),
        in_specs=[pl.BlockSpec((bm, k), lambda i, j: (i, 0), pipeline_mode=pl.Buffered(1)),
                  pl.BlockSpec((None, k, bn), lambda i, j: (layer, 0, j)),
                  pl.BlockSpec((None, k, bn), lambda i, j: (layer, 0, j))],
        out_specs=pl.BlockSpec((bm, bn), lambda i, j: (i, j)),
        out_shape=jax.ShapeDtypeStruct((m, n), BF16),
        compiler_params=_cparams("parallel", "arbitrary"),
        name="ffn_gate_up",
    )(x, wg, wu)


def _mm_res_kernel(x_ref, w_ref, r_ref, o_ref):
    o_ref[...] = r_ref[...] + _dot(x_ref[...], w_ref[...].astype(BF16))


def matmul_residual(x, w, layer, res, bm_target=1040, bn_target=256):
    m, k = x.shape
    n = w.shape[2]
    bm = _pick(m, bm_target, 16)
    bn = _pick(n, bn_target, LANES)
    return pl.pallas_call(
        _mm_res_kernel,
        grid=(m // bm, n // bn),
        in_specs=[pl.BlockSpec((bm, k), lambda i, j: (i, 0), pipeline_mode=pl.Buffered(1)),
                  pl.BlockSpec((None, k, bn), lambda i, j: (layer, 0, j)),
                  pl.BlockSpec((bm, bn), lambda i, j: (i, j))],
        out_specs=pl.BlockSpec((bm, bn), lambda i, j: (i, j)),
        out_shape=jax.ShapeDtypeStruct((m, n), F32),
        compiler_params=_cparams("parallel", "arbitrary"),
        name="ffn_down",
    )(x, w, res)


def _head_sum(x, bd2):
    cols = x.shape[1] // LANES
    hi, lo = _split(x)
    parts = [_dot(jnp.concatenate([hi[:, c * LANES:(c + 1) * LANES], lo[:, c * LANES:(c + 1) * LANES]], axis=1), bd2)
             for c in range(cols)]
    return parts[0] if cols == 1 else jnp.concatenate(parts, axis=1)


def _dot3_pre(a, b3):
    return _dot(_l3(a, 1), b3)


def _r3_rows(w):
    return _r3(w.astype(F32), 0)


def _rwkv_prep_kernel(*refs, dr, wins, has_vgate, tr, seq, prompt_tiles):
    it = iter(refs)
    p_ref, halo_ref, s0_ref, ps_ref, mu_ref = next(it), next(it), next(it), next(it), next(it)
    w0_ref, w2_ref, a0_ref, a2_ref, g2_ref = next(it), next(it), next(it), next(it), next(it)
    kk_ref, ka_ref, rk_ref, bd_ref = next(it), next(it), next(it), next(it)
    if has_vgate:
        vf_ref, v0_ref, v1_ref, v2_ref = next(it), next(it), next(it), next(it)
    r_o, lw_o, k_o, v_o, kk_o, be_o, g_o, bo_o = (next(it) for _ in range(8))

    i = pl.program_id(0)
    p = p_ref[...]
    first = jnp.where((i * tr) % seq == 0, s0_ref[0], halo_ref[SUBLANES - 1:SUBLANES, :])
    rowid = lax.broadcasted_iota(jnp.int32, (tr, 1), 0)
    prev = jnp.where(rowid == 0, first, pltpu.roll(p, 1, axis=0))
    prev = jnp.where(i >= prompt_tiles, ps_ref[...], prev)

    xs = p + (prev - p) * mu_ref[...]
    xr = xs[:, 0:dr]
    xk = xs[:, dr:2 * dr]
    xv = xs[:, 2 * dr:3 * dr]
    xw, xa, xg = (xs[:, lo:hi] for lo, hi in wins)

    d = w0_ref[...] + _dot3_pre(jnp.tanh(xw), w2_ref[...])
    nd = -d
    softplus = jnp.maximum(nd, 0.0) + jnp.log(1.0 + jnp.exp(-jnp.abs(nd)))
    lw = -jnp.exp(-softplus - 0.5)
    a = _sigmoid(a0_ref[...] + _dot3_pre(xa, a2_ref[...]))
    g = _dot3_pre(_sigmoid(xg), g2_ref[...])
    if has_vgate:
        vg = _sigmoid(v0_ref[...] + _dot3_pre(_dot3_pre(xv, v1_ref[...]), v2_ref[...]))
        v = xv + (vf_ref[...] - xv) * vg
    else:
        v = xv
    bd = bd_ref[...]
    kk = xk * kk_ref[...]
    kk = kk / jnp.maximum(jnp.sqrt(_head_sum(kk * kk, bd)), 1e-12)
    k = xk * (1.0 + (a - 1.0) * ka_ref[...])
    bonus = _head_sum(xr * k * rk_ref[...], bd) * v

    r_o[...] = xr
    lw_o[...] = lw
    k_o[...] = k
    v_o[...] = v
    kk_o[...] = kk
    be_o[...] = kk * a
    g_o[...] = g
    bo_o[...] = bonus


def rwkv_prep(proj, shift0_p, shift_s, mu, w0, w2, a0, a2, g2, k_k, k_a, r_k, bd, vgate, dims):
    dr, wins, rpp, batch, seq = dims
    m = proj.shape[0]
    dec = shift_s.shape[0]
    mp = batch * seq
    tr = _pick(math.gcd(seq, dec), 128, SUBLANES)
    pt = mp // tr
    row = lambda i: (i, 0)
    vec = lambda x: x.reshape(1, -1)
    const = lambda shape: pl.BlockSpec(shape, lambda i: (0, 0), pipeline_mode=pl.Buffered(1))
    args = [proj, proj, shift0_p, shift_s, vec(mu), vec(w0), w2, vec(a0), a2, g2, vec(k_k), vec(k_a), vec(r_k), bd]
    specs = [pl.BlockSpec((tr, rpp), row),
             pl.BlockSpec((SUBLANES, rpp), lambda i: (jnp.maximum(i * (tr // SUBLANES) - 1, 0), 0)),
             pl.BlockSpec((1, 1, rpp), lambda i: (jnp.minimum((i * tr) // seq, batch - 1), 0, 0)),
             pl.BlockSpec((tr, rpp), lambda i: (jnp.maximum(i - pt, 0), 0)),
             const((1, rpp)),
             const((1, dr)), const(w2.shape), const((1, dr)),
             const(a2.shape), const(g2.shape), const((1, dr)),
             const((1, dr)), const((1, dr)), const((2 * LANES, LANES))]
    if vgate is not None:
        v_first, v0, v1, v2 = vgate
        args += [v_first, vec(v0), v1, v2]
        specs += [pl.BlockSpec((tr, dr), row), const((1, dr)), const(v1.shape), const(v2.shape)]
    out = jax.ShapeDtypeStruct((m, dr), F32)
    return pl.pallas_call(
        functools.partial(_rwkv_prep_kernel, dr=dr, wins=wins, has_vgate=vgate is not None,
                          tr=tr, seq=seq, prompt_tiles=pt),
        grid=(m // tr,),
        in_specs=specs,
        out_specs=[pl.BlockSpec((tr, dr), row)] * 8,
        out_shape=[out] * 8,
        compiler_params=_cparams("parallel"),
        name="rwkv_prep",
    )(*args)


def _wkv_pairs(rb, kb, ab, bb, bh, kh, v, bd, cum, nh):
    C = CHUNK
    G = range(len(rb))
    lane3 = lax.broadcasted_iota(jnp.int32, (1, 6 * nh), 1) % (2 * nh)
    in_a3 = lane3 < nh
    lane = lax.broadcasted_iota(jnp.int32, (1, 2 * nh), 1)
    m_a = (lane < nh).astype(F32)
    m_b = 1.0 - m_a
    zb = jnp.zeros((), BF16)
    trow = lax.broadcasted_iota(jnp.int32, (C, 2 * C), 0)
    tcol = lax.broadcasted_iota(jnp.int32, (C, 2 * C), 1)
    first = tcol < C
    tcol = jnp.where(first, tcol, tcol - C)
    strict = trow > tcol
    incl = trow >= tcol

    def lhs_of(g):
        ab3 = _l3(ab[g], 1)
        rb3 = _l3(rb[g], 1)
        return jnp.concatenate([jnp.where(in_a3, ab3, zb), jnp.where(in_a3, zb, ab3),
                                jnp.where(in_a3, rb3, zb), jnp.where(in_a3, zb, rb3)], axis=0)

    p = [_dot_nt(lhs_of(g), jnp.concatenate([_r3(bb[g], 1), _r3(kb[g], 1), _r3(bd[g], 1)], axis=0)) for g in G]
    a_s = [p[g][0:C, 2 * C:] + p[g][C:2 * C, 2 * C:] for g in G]
    r_s = [p[g][2 * C:3 * C, 2 * C:] + p[g][3 * C:4 * C, 2 * C:] for g in G]
    pa = [jnp.where(strict, p[g][0:C, 0:2 * C], 0.0) for g in G]
    pb = [jnp.where(strict, p[g][C:2 * C, 0:2 * C], 0.0) for g in G]
    pr = [jnp.concatenate([jnp.where(incl, p[g][2 * C:3 * C, 0:2 * C], 0.0),
                           jnp.where(incl, p[g][3 * C:4 * C, 0:2 * C], 0.0)], axis=1) for g in G]

    v_a = [v[g] * m_a for g in G]
    v_b = [v[g] * m_b for g in G]
    ak = [jnp.where(first, pltpu.roll(pa[g], C, axis=1), pb[g]) for g in G]
    z = [_dot3(ak[g], jnp.concatenate([v_a[g], v_b[g]], axis=0)) for g in G]
    x = [a_s[g] + z[g] for g in G]

    mc = [jnp.where(first, pa[g], pltpu.roll(pb[g], C, axis=1)) for g in G]
    steps = C.bit_length() - 1
    for s in range(steps):
        xst = [jnp.concatenate([x[g] * m_a, x[g] * m_b], axis=0) for g in G]
        if s < steps - 1:
            mst = [jnp.concatenate([jnp.where(first, mc[g], 0.0), jnp.where(first, 0.0, mc[g])], axis=0) for g in G]
            out = [_dot3(mc[g], jnp.concatenate([mst[g], xst[g]], axis=1)) for g in G]
            mc = [out[g][:, 0:2 * C] for g in G]
            x = [x[g] + out[g][:, 2 * C:] for g in G]
        else:
            out = [_dot3(mc[g], xst[g]) for g in G]
            x = [x[g] + out[g] for g in G]
    u = x

    yo = [_dot3(pr[g], jnp.concatenate([u[g] * m_a, v_a[g], u[g] * m_b, v_b[g]], axis=0)) for g in G]
    y = [r_s[g] + yo[g] for g in G]

    brow = lax.broadcasted_iota(jnp.int32, (2 * nh, 2 * nh), 0) < nh
    bcol = lax.broadcasted_iota(jnp.int32, (2 * nh, 2 * nh), 1) < nh
    upd = [_dot3_tn(jnp.concatenate([u[g], v[g]], axis=0), jnp.concatenate([bh[g], kh[g]], axis=0)) for g in G]
    bd_new = [bd[g] * jnp.exp(cum[g][C - 1:C, :]) + jnp.where(brow == bcol, upd[g], 0.0) for g in G]
    return y, bd_new


def _wkv_chunk_kernel(r_ref, lw_ref, k_ref, v_ref, kk_ref, be_ref, s0_ref, y_ref, so_ref, bd_ref, *, nh, gp):
    c = pl.program_id(2)
    C = CHUNK
    w = 2 * nh

    @pl.when(c == 0)
    def _():
        bd_ref[...] = s0_ref[0]

    lw = lw_ref[...]
    k = k_ref[...]
    be = be_ref[...]
    cum = _cumsum_rows(lw)
    cl = cum[C - 1:C, :]
    wi = jnp.exp(-cum)
    we = jnp.exp(cl - cum)
    rb = r_ref[...] * jnp.exp(cum)
    kb = k * wi
    ab = -kk_ref[...] * jnp.exp(cum - lw)
    bb = be * wi
    bh = be * we
    kh = k * we
    v = v_ref[...]

    tiles = lambda t: [t[:, g * w:(g + 1) * w] for g in range(gp)]
    ys, bds = _wkv_pairs(tiles(rb), tiles(kb), tiles(ab), tiles(bb), tiles(bh), tiles(kh), tiles(v),
                         [bd_ref[g * w:(g + 1) * w, :] for g in range(gp)], tiles(cum), nh)
    y_ref[...] = ys[0] if gp == 1 else jnp.concatenate(ys, axis=1)
    bd_all = bds[0] if gp == 1 else jnp.concatenate(bds, axis=0)
    bd_ref[...] = bd_all

    @pl.when(c == pl.num_programs(2) - 1)
    def _():
        so_ref[0] = bd_all


def wkv_chunk(r, lw, k, v, kk, be, s0_bd, batch, seq, nh):
    dr = r.shape[1]
    w = 2 * nh
    pairs = dr // w
    gp = _pick(pairs, 8, 1)
    nc = seq // CHUNK
    tile = pl.BlockSpec((CHUNK, gp * w), lambda b, h, c: (b * nc + c, h))
    st = pl.BlockSpec((1, gp * w, w), lambda b, h, c: (b, h, 0))
    return pl.pallas_call(
        functools.partial(_wkv_chunk_kernel, nh=nh, gp=gp),
        grid=(batch, pairs // gp, nc),
        in_specs=[tile] * 6 + [st],
        out_specs=[tile, st],
        out_shape=[jax.ShapeDtypeStruct((batch * seq, dr), F32),
                   jax.ShapeDtypeStruct(s0_bd.shape, F32)],
        scratch_shapes=[pltpu.VMEM((gp * w, w), F32)],
        compiler_params=_cparams("parallel", "parallel", "arbitrary"),
        name="wkv7_chunk",
    )(r, lw, k, v, kk, be, s0_bd)


def _wkv_step_kernel(r_ref, lw_ref, k_ref, v_ref, kk_ref, be_ref, s_ref, *rest):
    *done_refs, y_ref, so_ref = rest
    s = s_ref[...]
    sa = jnp.sum(s * kk_ref[...], axis=2, keepdims=True)
    s_new = s * jnp.exp(lw_ref[...]) - sa * be_ref[...] + v_ref[...] * k_ref[...]
    y_ref[...] = jnp.sum(s_new * r_ref[...], axis=2, keepdims=True)
    _store_stacked(so_ref, done_refs, s_new)


def _store_stacked(so_ref, done_refs, s_new):
    if done_refs:
        for l, d_ref in enumerate(done_refs):
            so_ref[l] = d_ref[...]
        so_ref[len(done_refs)] = s_new
    else:
        so_ref[...] = s_new


def _state_specs(g, n, layer, done):
    st = pl.BlockSpec((g, n, n), lambda i: (i, 0, 0))
    st_in = pl.BlockSpec((None, g, n, n), lambda i: (layer, i, 0, 0))
    if done:
        return st_in, [st] * len(done), pl.BlockSpec((len(done) + 1, g, n, n), lambda i: (0, i, 0, 0))
    return st_in, [], st


def wkv_step(r, lw, k, v, kk, be, state, layer, done=()):
    _, h, n, _, b = state.shape
    g = _pick(h, 2, 1)
    row = pl.BlockSpec((g, 1, n, b), lambda i: (i, 0, 0, 0))
    col = pl.BlockSpec((g, n, 1, b), lambda i: (i, 0, 0, 0))
    st = pl.BlockSpec((g, n, n, b), lambda i: (i, 0, 0, 0))
    st_in = pl.BlockSpec((None, g, n, n, b), lambda i: (layer, i, 0, 0, 0))
    st_out = pl.BlockSpec((len(done) + 1, g, n, n, b), lambda i: (0, i, 0, 0, 0)) if done else st
    so_shape = ((len(done) + 1,) if done else ()) + (h, n, n, b)
    return pl.pallas_call(
        _wkv_step_kernel,
        grid=(h // g,),
        in_specs=[row, row, row, col, row, row, st_in] + [st] * len(done),
        out_specs=[col, st_out],
        out_shape=[jax.ShapeDtypeStruct((h, n, 1, b), F32), jax.ShapeDtypeStruct(so_shape, F32)],
        compiler_params=_cparams("parallel"),
        name="wkv7_step",
    )(r, lw, k, v, kk, be, state, *done)


def _rwkv_post_kernel(yp_ref, ys_ref, bo_ref, g_ref, w_ref, b_ref, bd_ref, o_ref, *, nh, prompt_tiles):
    y = jnp.where(pl.program_id(0) >= prompt_tiles, ys_ref[...], yp_ref[...])
    bd = bd_ref[...]
    mean = _head_sum(y, bd) * (1.0 / nh)
    d = y - mean
    var = _head_sum(d * d, bd) * (1.0 / nh)
    yn = d * lax.rsqrt(var + EPS_LNX) * w_ref[...] + b_ref[...]
    o_ref[...] = ((yn + bo_ref[...]) * g_ref[...]).astype(o_ref.dtype)


def rwkv_post(y_p, y_s, bonus, g, lnx_w, lnx_b, bd, nh):
    mp, dr = y_p.shape
    dec = y_s.shape[0]
    m = mp + dec
    tr = _pick(math.gcd(mp, dec), 256, 16)
    pt = mp // tr
    row = lambda i: (i, 0)
    fix = lambda i: (0, 0)
    return pl.pallas_call(
        functools.partial(_rwkv_post_kernel, nh=nh, prompt_tiles=pt),
        grid=(m // tr,),
        in_specs=[pl.BlockSpec((tr, dr), lambda i: (jnp.minimum(i, pt - 1), 0)),
                  pl.BlockSpec((tr, dr), lambda i: (jnp.maximum(i - pt, 0), 0)),
                  pl.BlockSpec((tr, dr), row), pl.BlockSpec((tr, dr), row),
                  pl.BlockSpec((1, dr), fix), pl.BlockSpec((1, dr), fix),
                  pl.BlockSpec((2 * LANES, LANES), fix)],
        out_specs=pl.BlockSpec((tr, dr), row),
        out_shape=jax.ShapeDtypeStruct((m, dr), BF16),
        compiler_params=_cparams("parallel"),
        name="rwkv_post",
    )(y_p, y_s, bonus, g, lnx_w.reshape(1, dr), lnx_b.reshape(1, dr), bd)


def _hgrn_gates(qh, fh, lb):
    f = lb + (1.0 - lb) * _sigmoid(fh)
    logf = jnp.log(jnp.maximum(f, MIN_FORGET))
    kin = (1.0 - lb) * _sigmoid(-fh)
    q = qh * _sigmoid(qh)
    return q, kin, logf


def _hgrn_out(o, gh, nw):
    o = o * lax.rsqrt(jnp.mean(o * o, axis=-1, keepdims=True) + EPS_HGRN)
    return o * nw * (gh * _sigmoid(gh))


def _hgrn_heads(q, kin, v, b, st):
    C = CHUNK
    H = range(len(q))
    o_inter = [_dot3_nt(q[h] * jnp.exp(b[h]), st[h]) for h in H]

    row = lax.broadcasted_iota(jnp.int32, (C, C), 0)
    col = lax.broadcasted_iota(jnp.int32, (C, C), 1)
    rsub = row % SUB
    scol = lax.broadcasted_iota(jnp.int32, (SUB, C), 1)
    offs = []
    for blk in range(1, C // SUB):
        lo = blk * SUB
        offs.append([_dot3_nt(q[h][lo:lo + SUB] * jnp.exp(b[h][lo:lo + SUB] - b[h][lo - 1:lo]),
                              kin[h] * jnp.exp(jnp.minimum(b[h][lo - 1:lo] - b[h], 0.0))) for h in H])
    a = []
    for h in H:
        diag = jnp.zeros((C, C), F32)
        bk = b[h] - jnp.log(kin[h])
        for dlt in range(SUB):
            if dlt == 0:
                w = q[h] * kin[h]
            else:
                w = q[h] * jnp.exp(jnp.minimum(b[h] - pltpu.roll(bk, dlt, axis=0), 0.0))
            colsum = jnp.sum(w, axis=-1, keepdims=True)
            diag = jnp.where((col == row - dlt) & (rsub >= dlt), colsum, diag)
        blocks = [diag[0:SUB]]
        for blk in range(1, C // SUB):
            lo = blk * SUB
            blocks.append(jnp.where(scol < lo, offs[blk - 1][h], diag[lo:lo + SUB]))
        a.append(jnp.concatenate(blocks, axis=0))
    o_intra = [_dot3(a[h], v[h]) for h in H]
    upd = [_dot3_tn(v[h], kin[h] * jnp.exp(b[h][C - 1:C, :] - b[h])) for h in H]
    o = [o_inter[h] + o_intra[h] for h in H]
    st_new = [st[h] * jnp.exp(b[h][C - 1:C, :]) + upd[h] for h in H]
    return o, st_new


def _hgrn_chunk_kernel(q_ref, f_ref, i_ref, g_ref, lb_ref, nw_ref, s0_ref, o_ref, so_ref, st_ref, *, dh, gh):
    c = pl.program_id(2)

    @pl.when(c == 0)
    def _():
        for h in range(gh):
            st_ref[h * dh:(h + 1) * dh, :] = s0_ref[0, h].T

    q, kin, logf = _hgrn_gates(q_ref[...], f_ref[...], lb_ref[...])
    v = i_ref[...]
    b = _cumsum_rows(logf)
    tiles = lambda t: [t[:, h * dh:(h + 1) * dh] for h in range(gh)]
    os, sts = _hgrn_heads(tiles(q), tiles(kin), tiles(v), tiles(b),
                          [st_ref[h * dh:(h + 1) * dh, :] for h in range(gh)])
    os = [o * lax.rsqrt(jnp.mean(o * o, axis=-1, keepdims=True) + EPS_HGRN) for o in os]
    o = os[0] if gh == 1 else jnp.concatenate(os, axis=1)
    gate = g_ref[...]
    o_ref[...] = (o * nw_ref[...] * (gate * _sigmoid(gate))).astype(o_ref.dtype)
    st_ref[...] = sts[0] if gh == 1 else jnp.concatenate(sts, axis=0)

    @pl.when(c == pl.num_programs(2) - 1)
    def _():
        for h in range(gh):
            so_ref[0, h] = sts[h].T


def hgrn_chunk(proj, lb, nw, s0, batch, seq, heads, dh):
    nc = seq // CHUNK
    gh = _pick(heads, 4, 1)
    wide = gh * dh
    hb = heads // gh

    def seg(j):
        return pl.BlockSpec((CHUNK, wide), lambda b, h, c: (b * nc + c, j * hb + h))

    vec = pl.BlockSpec((1, wide), lambda b, h, c: (0, h))
    st = pl.BlockSpec((1, gh, dh, dh), lambda b, h, c: (b, h, 0, 0))
    return pl.pallas_call(
        functools.partial(_hgrn_chunk_kernel, dh=dh, gh=gh),
        grid=(batch, hb, nc),
        in_specs=[seg(0), seg(1), seg(2), seg(3), vec, vec, st],
        out_specs=[pl.BlockSpec((CHUNK, wide), lambda b, h, c: (b * nc + c, h)), st],
        out_shape=[jax.ShapeDtypeStruct((batch * seq, heads * dh), BF16),
                   jax.ShapeDtypeStruct(s0.shape, F32)],
        scratch_shapes=[pltpu.VMEM((wide, dh), F32)],
        compiler_params=_cparams("parallel", "parallel", "arbitrary"),
        name="hgrn2_chunk",
    )(proj, proj, proj, proj, lb.reshape(1, -1), nw.reshape(1, -1), s0)


def _hgrn_step_kernel(q_ref, f_ref, i_ref, g_ref, lb_ref, nw_ref, s_ref, *rest):
    *done_refs, o_ref, so_ref = rest
    s = s_ref[...]
    n = s.shape[-1]
    q, kin, logf = _hgrn_gates(q_ref[...], f_ref[...], lb_ref[...])
    eye = (lax.broadcasted_iota(jnp.int32, (n, n), 0) == lax.broadcasted_iota(jnp.int32, (n, n), 1))[None]

    def col(x):
        return jnp.sum(jnp.where(eye, x, 0.0), axis=-1, keepdims=True)

    s_new = s * col(jnp.exp(logf)) + col(kin) * i_ref[...]
    o = jnp.sum(s_new * col(q), axis=-2, keepdims=True)
    o_ref[...] = _hgrn_out(o, g_ref[...], nw_ref[...]).astype(o_ref.dtype)
    _store_stacked(so_ref, done_refs, s_new)


def hgrn_step(qh, fh, ih, gh, lb, nw, state, layer, heads, done=()):
    _, bh, n, _ = state.shape
    g = _pick(heads, 16, 1)
    hb = heads // g
    vec = pl.BlockSpec((g, 1, n), lambda i: (i, 0, 0))
    par = pl.BlockSpec((g, 1, n), lambda i: (i % hb, 0, 0))
    st_in, done_specs, st_out = _state_specs(g, n, layer, done)
    so_shape = (len(done) + 1, bh, n, n) if done else (bh, n, n)
    return pl.pallas_call(
        _hgrn_step_kernel,
        grid=(bh // g,),
        in_specs=[vec] * 4 + [par, par, st_in] + done_specs,
        out_specs=[vec, st_out],
        out_shape=[jax.ShapeDtypeStruct((bh, 1, n), BF16), jax.ShapeDtypeStruct(so_shape, F32)],
        compiler_params=_cparams("parallel"),
        name="hgrn2_step",
    )(qh, fh, ih, gh, lb, nw, state, *done)


def _window(lo, hi):
    return (lo // LANES) * LANES, _round_up(hi, LANES)


def _window_rows(w, lo, hi):
    ws, we = _window(lo, hi)
    return jnp.pad(w, ((lo - ws, we - hi), (0, 0)))


def kernel(x_prompt, x_sample, state_wkv, state_hgrn, state_shift, norm_mix, w_in, mu_shift, w0, w2, a0, a2, v0, v1, v2, g2, k_k, k_a, r_k, lnx_w, lnx_b, hgrn_lb, hgrn_norm, w_out, norm_ffn, w_gate, w_up, w_down, norm_final):
    batch, seq, dm = x_prompt.shape
    dec = x_sample.shape[0]
    depth = w_in.shape[0]
    _, _, rh, nh, _ = state_wkv.shape
    _, _, hh, dh, _ = state_hgrn.shape
    dr, dhg = rh * nh, hh * dh
    rw, ra, rg = w2.shape[1], a2.shape[1], g2.shape[1]
    rp = 3 * dr + rw + ra + rg
    n_in = rp + 4 * dhg
    assert x_sample.shape[1] == 1 and seq % CHUNK == 0 and CHUNK % SUB == 0
    assert 2 * nh == LANES and dh == LANES and rh % 2 == 0 and dr % LANES == 0
    assert w_in.shape[2] == n_in and state_shift.shape[2] == rp
    seg_w = (3 * dr, 3 * dr + rw)
    seg_a = (seg_w[1], seg_w[1] + ra)
    seg_g = (seg_a[1], rp)
    rpp = _round_up(rp, LANES)
    mp = batch * seq
    dims = (dr, (_window(*seg_w), _window(*seg_a), _window(*seg_g)), rpp, batch, seq)
    pad_rp = lambda t: jnp.pad(t, [(0, 0)] * (t.ndim - 1) + [(0, rpp - rp)])

    pz = jax.nn.softmax(hgrn_lb.astype(F32), axis=0)
    lbs = jnp.cumsum(pz, axis=0) - pz[0:1]

    lane_head = jnp.arange(LANES) // nh
    bd = (lane_head[:, None] == lane_head[None, :]).astype(BF16)
    bd = jnp.concatenate([bd, bd], axis=0)
    w_in_t = jnp.swapaxes(w_in, 1, 2)
    w_r_t = jnp.pad(w_in_t[:, :rp], ((0, 0), (0, rpp - rp), (0, 0))).astype(BF16)
    w_h_t = w_in_t[:, rp:].astype(BF16)
    w_down_bf = w_down.astype(BF16)
    state_wkv_t = jnp.transpose(state_wkv, (0, 2, 3, 4, 1))
    state_hgrn_f = state_hgrn.reshape(depth, dec * hh, dh, dh)

    x = jnp.concatenate([x_prompt.reshape(mp, dm), x_sample.reshape(dec, dm)], axis=0)
    zero_bd = jnp.zeros((batch, (rh // 2) * LANES, LANES), F32)
    zero_hg = jnp.zeros((batch, hh, dh, dh), F32)
    zero_shift = jnp.zeros((batch, 1, rpp), F32)

    wkv_p, hgrn_p, shift_p, wkv_s, hgrn_s, shift_s = [], [], [], [], [], []
    v_first = None
    for l in range(depth):
        h = rmsnorm(x, norm_mix[l], BF16)
        proj = matmul_nt(h, w_r_t, l, F32)
        proj_h = matmul_nt(h, w_h_t, l, F32)

        vgate = None if l == 0 else (v_first, v0[l - 1], _r3_rows(v1[l - 1]), _r3_rows(v2[l - 1]))
        r, lw, k, v, kk, be, g, bonus = rwkv_prep(
            proj, zero_shift, pad_rp(state_shift[l]), pad_rp(mu_shift[l]), w0[l],
            _r3_rows(_window_rows(w2[l], *seg_w)), a0[l], _r3_rows(_window_rows(a2[l], *seg_a)),
            _r3_rows(_window_rows(g2[l], *seg_g)), k_k[l], k_a[l], r_k[l].reshape(-1), bd, vgate, dims)
        if l == 0:
            v_first = v
        y_p, bd_out = wkv_chunk(r, lw, k, v, kk, be, zero_bd, batch, seq, nh)
        rows = lambda t: t[mp:].T.reshape(rh, 1, nh, dec)
        y_s, wkv_new_s = wkv_step(rows(r), rows(lw), rows(k), v[mp:].T.reshape(rh, nh, 1, dec), rows(kk), rows(be),
                                  state_wkv_t, l, wkv_s if l == depth - 1 else ())
        y_rwkv = rwkv_post(y_p, y_s.reshape(dr, dec).T, bonus, g, lnx_w[l], lnx_b[l], bd, nh)
        bd4 = bd_out.reshape(batch, rh // 2, LANES, LANES)
        wkv_p.append(jnp.stack([bd4[:, :, :nh, :nh], bd4[:, :, nh:, nh:]], axis=2).reshape(batch, rh, nh, nh))
        wkv_s.append(wkv_new_s)
        shift_p.append(jnp.concatenate([proj[(b + 1) * seq - 1:(b + 1) * seq, :rp] for b in range(batch)], axis=0))
        shift_s.append(proj[mp:, :rp])

        o_p, hg_out = hgrn_chunk(proj_h, lbs[l], hgrn_norm[l], zero_hg, batch, seq, hh, dh)
        ph_s = proj_h[mp:].reshape(dec, 4, hh, dh).transpose(1, 0, 2, 3).reshape(4, dec * hh, 1, dh)
        o_s, hgrn_new_s = hgrn_step(ph_s[0], ph_s[1], ph_s[2], ph_s[3], lbs[l].reshape(hh, 1, dh),
                                    hgrn_norm[l].reshape(hh, 1, dh), state_hgrn_f, l, hh,
                                    hgrn_s if l == depth - 1 else ())
        y_hgrn = jnp.concatenate([o_p, o_s.reshape(dec, dhg)], axis=0)
        hgrn_p.append(hg_out)
        hgrn_s.append(hgrn_new_s)

        x = matmul2_residual(y_rwkv, y_hgrn, w_out, l, x)

        h = rmsnorm(x, norm_ffn[l], BF16)
        act = gate_up(h, w_gate, w_up, l)
        x = matmul_residual(act, w_down_bf, l, x)

    y_p = rmsnorm(x, norm_final, F32, 0, mp)
    y_s = rmsnorm(x, norm_final, F32, mp, dec)
    stacked = lambda t, shape: (t[-1] if depth > 1 else t[0][None]).reshape((depth,) + shape)
    wkv_s_out = jnp.transpose(stacked(wkv_s, (rh, nh, nh, dec)), (0, 4, 1, 2, 3))
    return (y_p.reshape(batch, seq, dm), y_s.reshape(dec, 1, dm),
            jnp.stack(wkv_p), jnp.stack(hgrn_p), jnp.stack(shift_p),
            wkv_s_out, stacked(hgrn_s, (dec, hh, dh, dh)), jnp.stack(shift_s))
```

```python
import functools
import math

import jax
import jax.numpy as jnp
from jax import lax
from jax.experimental import pallas as pl
from jax.experimental.pallas import tpu as pltpu

F32 = jnp.float32
BF16 = jnp.bfloat16
HI = lax.Precision.HIGHEST

LANES = 128
SUBLANES = 8
VMEM_LIMIT = 56 * 1024 * 1024

CHUNK = 64
SUB = 16
EPS_RMS = 1e-6
EPS_LNX = 64e-5
EPS_HGRN = 1e-5
MIN_FORGET = 1e-30


def _round_up(x, m):
    return -(-x // m) * m


def _pick(n, target, align):
    best = None
    for d in range(align, min(n, target) + 1, align):
        if n % d == 0:
            best = d
    return best if best is not None else n


def _cparams(*sem):
    return pltpu.CompilerParams(dimension_semantics=sem, vmem_limit_bytes=VMEM_LIMIT)


def _sigmoid(x):
    return 1.0 / (1.0 + jnp.exp(-x))


def _dot(a, b, precision=None):
    return jnp.dot(a, b, preferred_element_type=F32, precision=precision)


def _dot_nt(a, b, precision=None):
    return lax.dot_general(a, b, (((1,), (1,)), ((), ())), preferred_element_type=F32, precision=precision)


def _dot_tn(a, b, precision=None):
    return lax.dot_general(a, b, (((0,), (0,)), ((), ())), preferred_element_type=F32, precision=precision)


def _split(x):
    hi = x.astype(BF16)
    lo = (x - hi.astype(F32)).astype(BF16)
    return hi, lo


def _l3(x, axis):
    hi, lo = _split(x)
    return jnp.concatenate([hi, hi, lo], axis=axis)


def _r3(x, axis):
    hi, lo = _split(x)
    return jnp.concatenate([hi, lo, hi], axis=axis)


def _dot3(a, b):
    return _dot(_l3(a, 1), _r3(b, 0))


def _dot3_nt(a, b):
    return _dot_nt(_l3(a, 1), _r3(b, 1))


def _dot3_tn(a, b):
    return _dot_tn(_l3(a, 0), _r3(b, 0))


def _cumsum_rows(x):
    c = x.shape[0]
    n = x.shape[1]
    row = lax.broadcasted_iota(jnp.int32, (c, c), 0)
    col = lax.broadcasted_iota(jnp.int32, (c, c), 1)
    ltri = (row >= col).astype(BF16)
    hi = x.astype(BF16)
    r1 = x - hi.astype(F32)
    mid = r1.astype(BF16)
    lo = (r1 - mid.astype(F32)).astype(BF16)
    out = _dot(ltri, jnp.concatenate([hi, mid, lo], axis=1))
    return out[:, 0:n] + out[:, n:2 * n] + out[:, 2 * n:3 * n]


def _rmsnorm_kernel(x_ref, g_ref, o_ref):
    x = x_ref[...]
    ms = jnp.mean(x * x, axis=-1, keepdims=True)
    o_ref[...] = (x * lax.rsqrt(ms + EPS_RMS) * g_ref[...]).astype(o_ref.dtype)


def rmsnorm(x, g, out_dtype, row0=0, rows=None):
    d = x.shape[1]
    rows = x.shape[0] if rows is None else rows
    bm = _pick(math.gcd(rows, row0) if row0 else rows, 320, 16)
    off = row0 // bm
    return pl.pallas_call(
        _rmsnorm_kernel,
        grid=(rows // bm,),
        in_specs=[pl.BlockSpec((bm, d), lambda i: (i + off, 0)),
                  pl.BlockSpec((1, d), lambda i: (0, 0))],
        out_specs=pl.BlockSpec((bm, d), lambda i: (i, 0)),
        out_shape=jax.ShapeDtypeStruct((rows, d), out_dtype),
        compiler_params=_cparams("parallel"),
        name="rmsnorm",
    )(x, g.reshape(1, d))


def _mm_nt_kernel(x_ref, wt_ref, o_ref):
    o_ref[...] = _dot_nt(x_ref[...], wt_ref[...]).astype(o_ref.dtype)


def matmul_nt(x, wt, layer, out_dtype, bm_target=1040, bn_target=512):
    m, k = x.shape
    n = wt.shape[1]
    bm = _pick(m, bm_target, 16)
    bn = _pick(n, bn_target, LANES)
    return pl.pallas_call(
        _mm_nt_kernel,
        grid=(m // bm, n // bn),
        in_specs=[pl.BlockSpec((bm, k), lambda i, j: (i, 0)),
                  pl.BlockSpec((None, bn, k), lambda i, j: (layer, j, 0))],
        out_specs=pl.BlockSpec((bm, bn), lambda i, j: (i, j)),
        out_shape=jax.ShapeDtypeStruct((m, n), out_dtype),
        compiler_params=_cparams("parallel", "arbitrary"),
        name="matmul_nt",
    )(x, wt)


def _mm2_res_kernel(x1_ref, x2_ref, w_ref, r_ref, o_ref, *, k1):
    acc = _dot(x1_ref[...], w_ref[0:k1, :].astype(BF16))
    acc = acc + _dot(x2_ref[...], w_ref[k1:, :].astype(BF16))
    o_ref[...] = r_ref[...] + acc


def matmul2_residual(x1, x2, w, layer, res, bm_target=1040, bn_target=512):
    m, k1 = x1.shape
    k2 = x2.shape[1]
    n = w.shape[2]
    bm = _pick(m, bm_target, 16)
    bn = _pick(n, bn_target, LANES)
    return pl.pallas_call(
        functools.partial(_mm2_res_kernel, k1=k1),
        grid=(m // bm, n // bn),
        in_specs=[pl.BlockSpec((bm, k1), lambda i, j: (i, 0)),
                  pl.BlockSpec((bm, k2), lambda i, j: (i, 0)),
                  pl.BlockSpec((None, k1 + k2, bn), lambda i, j: (layer, 0, j)),
                  pl.BlockSpec((bm, bn), lambda i, j: (i, j))],
        out_specs=pl.BlockSpec((bm, bn), lambda i, j: (i, j)),
        out_shape=jax.ShapeDtypeStruct((m, n), F32),
        compiler_params=_cparams("parallel", "arbitrary"),
        name="out_proj",
    )(x1, x2, w, res)


def _gateup_kernel(x_ref, wg_ref, wu_ref, o_ref):
    x = x_ref[...]
    g = _dot(x, wg_ref[...].astype(BF16))
    u = _dot(x, wu_ref[...].astype(BF16))
    o_ref[...] = (g * _sigmoid(g) * u).astype(o_ref.dtype)


def gate_up(x, wg, wu, layer, bm_target=2080, bn_target=256):
    m, k = x.shape
    n = wg.shape[2]
    bm = _pick(m, bm_target, 16)
    bn = _pick(n, bn_target, LANES)
    return pl.pallas_call(
        _gateup_kernel,
        grid=(m // bm, n // bn),
        in_specs=[pl.BlockSpec((bm, k), lambda i, j: (i, 0), pipeline_mode=pl.Buffered(1)),
                  pl.BlockSpec((None, k, bn), lambda i, j: (layer, 0, j)),
                  pl.BlockSpec((None, k, bn), lambda i, j: (layer, 0, j))],
        out_specs=pl.BlockSpec((bm, bn), lambda i, j: (i, j)),
        out_shape=jax.ShapeDtypeStruct((m, n), BF16),
        compiler_params=_cparams("parallel", "arbitrary"),
        name="ffn_gate_up",
    )(x, wg, wu)


def _mm_res_kernel(x_ref, w_ref, r_ref, o_ref):
    o_ref[...] = r_ref[...] + _dot(x_ref[...], w_ref[...].astype(BF16))


def matmul_residual(x, w, layer, res, bm_target=1040, bn_target=256):
    m, k = x.shape
    n = w.shape[2]
    bm = _pick(m, bm_target, 16)
    bn = _pick(n, bn_target, LANES)
    return pl.pallas_call(
        _mm_res_kernel,
        grid=(m // bm, n // bn),
        in_specs=[pl.BlockSpec((bm, k), lambda i, j: (i, 0), pipeline_mode=pl.Buffered(1)),
                  pl.BlockSpec((None, k, bn), lambda i, j: (layer, 0, j)),
                  pl.BlockSpec((bm, bn), lambda i, j: (i, j))],
        out_specs=pl.BlockSpec((bm, bn), lambda i, j: (i, j)),
        out_shape=jax.ShapeDtypeStruct((m, n), F32),
        compiler_params=_cparams("parallel", "arbitrary"),
        name="ffn_down",
    )(x, w, res)


def _head_sum(x, bd2):
    cols = x.shape[1] // LANES
    hi, lo = _split(x)
    parts = [_dot(jnp.concatenate([hi[:, c * LANES:(c + 1) * LANES], lo[:, c * LANES:(c + 1) * LANES]], axis=1), bd2)
             for c in range(cols)]
    return parts[0] if cols == 1 else jnp.concatenate(parts, axis=1)


def _dot3_pre(a, b3):
    return _dot(_l3(a, 1), b3)


def _r3_rows(w):
    return _r3(w.astype(F32), 0)


def _rwkv_prep_kernel(*refs, dr, wins, has_vgate, tr, seq, prompt_tiles):
    it = iter(refs)
    p_ref, halo_ref, s0_ref, ps_ref, mu_ref = next(it), next(it), next(it), next(it), next(it)
    w0_ref, w2_ref, a0_ref, a2_ref, g2_ref = next(it), next(it), next(it), next(it), next(it)
    kk_ref, ka_ref, rk_ref, bd_ref = next(it), next(it), next(it), next(it)
    if has_vgate:
        vf_ref, v0_ref, v1_ref, v2_ref = next(it), next(it), next(it), next(it)
    r_o, lw_o, k_o, v_o, kk_o, be_o, g_o, bo_o = (next(it) for _ in range(8))

    i = pl.program_id(0)
    p = p_ref[...]
    first = jnp.where((i * tr) % seq == 0, s0_ref[0], halo_ref[SUBLANES - 1:SUBLANES, :])
    rowid = lax.broadcasted_iota(jnp.int32, (tr, 1), 0)
    prev = jnp.where(rowid == 0, first, pltpu.roll(p, 1, axis=0))
    prev = jnp.where(i >= prompt_tiles, ps_ref[...], prev)

    xs = p + (prev - p) * mu_ref[...]
    xr = xs[:, 0:dr]
    xk = xs[:, dr:2 * dr]
    xv = xs[:, 2 * dr:3 * dr]
    xw, xa, xg = (xs[:, lo:hi] for lo, hi in wins)

    d = w0_ref[...] + _dot3_pre(jnp.tanh(xw), w2_ref[...])
    nd = -d
    softplus = jnp.maximum(nd, 0.0) + jnp.log(1.0 + jnp.exp(-jnp.abs(nd)))
    lw = -jnp.exp(-softplus - 0.5)
    a = _sigmoid(a0_ref[...] + _dot3_pre(xa, a2_ref[...]))
    g = _dot3_pre(_sigmoid(xg), g2_ref[...])
    if has_vgate:
        vg = _sigmoid(v0_ref[...] + _dot3_pre(_dot3_pre(xv, v1_ref[...]), v2_ref[...]))
        v = xv + (vf_ref[...] - xv) * vg
    else:
        v = xv
    bd = bd_ref[...]
    kk = xk * kk_ref[...]
    kk = kk / jnp.maximum(jnp.sqrt(_head_sum(kk * kk, bd)), 1e-12)
    k = xk * (1.0 + (a - 1.0) * ka_ref[...])
    bonus = _head_sum(xr * k * rk_ref[...], bd) * v

    r_o[...] = xr
    lw_o[...] = lw
    k_o[...] = k
    v_o[...] = v
    kk_o[...] = kk
    be_o[...] = kk * a
    g_o[...] = g
    bo_o[...] = bonus


def rwkv_prep(proj, shift0_p, shift_s, mu, w0, w2, a0, a2, g2, k_k, k_a, r_k, bd, vgate, dims):
    dr, wins, rpp, batch, seq = dims
    m = proj.shape[0]
    dec = shift_s.shape[0]
    mp = batch * seq
    tr = _pick(math.gcd(seq, dec), 128, SUBLANES)
    pt = mp // tr
    row = lambda i: (i, 0)
    vec = lambda x: x.reshape(1, -1)
    const = lambda shape: pl.BlockSpec(shape, lambda i: (0, 0), pipeline_mode=pl.Buffered(1))
    args = [proj, proj, shift0_p, shift_s, vec(mu), vec(w0), w2, vec(a0), a2, g2, vec(k_k), vec(k_a), vec(r_k), bd]
    specs = [pl.BlockSpec((tr, rpp), row),
             pl.BlockSpec((SUBLANES, rpp), lambda i: (jnp.maximum(i * (tr // SUBLANES) - 1, 0), 0)),
             pl.BlockSpec((1, 1, rpp), lambda i: (jnp.minimum((i * tr) // seq, batch - 1), 0, 0)),
             pl.BlockSpec((tr, rpp), lambda i: (jnp.maximum(i - pt, 0), 0)),
             const((1, rpp)),
             const((1, dr)), const(w2.shape), const((1, dr)),
             const(a2.shape), const(g2.shape), const((1, dr)),
             const((1, dr)), const((1, dr)), const((2 * LANES, LANES))]
    if vgate is not None:
        v_first, v0, v1, v2 = vgate
        args += [v_first, vec(v0), v1, v2]
        specs += [pl.BlockSpec((tr, dr), row), const((1, dr)), const(v1.shape), const(v2.shape)]
    out = jax.ShapeDtypeStruct((m, dr), F32)
    return pl.pallas_call(
        functools.partial(_rwkv_prep_kernel, dr=dr, wins=wins, has_vgate=vgate is not None,
                          tr=tr, seq=seq, prompt_tiles=pt),
        grid=(m // tr,),
        in_specs=specs,
        out_specs=[pl.BlockSpec((tr, dr), row)] * 8,
        out_shape=[out] * 8,
        compiler_params=_cparams("parallel"),
        name="rwkv_prep",
    )(*args)


def _wkv_pairs(rb, kb, ab, bb, bh, kh, v, bd, cum, nh):
    C = CHUNK
    G = range(len(rb))
    lane3 = lax.broadcasted_iota(jnp.int32, (1, 6 * nh), 1) % (2 * nh)
    in_a3 = lane3 < nh
    lane = lax.broadcasted_iota(jnp.int32, (1, 2 * nh), 1)
    m_a = (lane < nh).astype(F32)
    m_b = 1.0 - m_a
    zb = jnp.zeros((), BF16)
    trow = lax.broadcasted_iota(jnp.int32, (C, 2 * C), 0)
    tcol = lax.broadcasted_iota(jnp.int32, (C, 2 * C), 1)
    first = tcol < C
    tcol = jnp.where(first, tcol, tcol - C)
    strict = trow > tcol
    incl = trow >= tcol

    def lhs_of(g):
        ab3 = _l3(ab[g], 1)
        rb3 = _l3(rb[g], 1)
        return jnp.concatenate([jnp.where(in_a3, ab3, zb), jnp.where(in_a3, zb, ab3),
                                jnp.where(in_a3, rb3, zb), jnp.where(in_a3, zb, rb3)], axis=0)

    p = [_dot_nt(lhs_of(g), jnp.concatenate([_r3(bb[g], 1), _r3(kb[g], 1), _r3(bd[g], 1)], axis=0)) for g in G]
    a_s = [p[g][0:C, 2 * C:] + p[g][C:2 * C, 2 * C:] for g in G]
    r_s = [p[g][2 * C:3 * C, 2 * C:] + p[g][3 * C:4 * C, 2 * C:] for g in G]
    pa = [jnp.where(strict, p[g][0:C, 0:2 * C], 0.0) for g in G]
    pb = [jnp.where(strict, p[g][C:2 * C, 0:2 * C], 0.0) for g in G]
    pr = [jnp.concatenate([jnp.where(incl, p[g][2 * C:3 * C, 0:2 * C], 0.0),
                           jnp.where(incl, p[g][3 * C:4 * C, 0:2 * C], 0.0)], axis=1) for g in G]

    v_a = [v[g] * m_a for g in G]
    v_b = [v[g] * m_b for g in G]
    ak = [jnp.where(first, pltpu.roll(pa[g], C, axis=1), pb[g]) for g in G]
    z = [_dot3(ak[g], jnp.concatenate([v_a[g], v_b[g]], axis=0)) for g in G]
    x = [a_s[g] + z[g] for g in G]

    mc = [jnp.where(first, pa[g], pltpu.roll(pb[g], C, axis=1)) for g in G]
    steps = C.bit_length() - 1
    for s in range(steps):
        xst = [jnp.concatenate([x[g] * m_a, x[g] * m_b], axis=0) for g in G]
        if s < steps - 1:
            mst = [jnp.concatenate([jnp.where(first, mc[g], 0.0), jnp.where(first, 0.0, mc[g])], axis=0) for g in G]
            out = [_dot3(mc[g], jnp.concatenate([mst[g], xst[g]], axis=1)) for g in G]
            mc = [out[g][:, 0:2 * C] for g in G]
            x = [x[g] + out[g][:, 2 * C:] for g in G]
        else:
            out = [_dot3(mc[g], xst[g]) for g in G]
            x = [x[g] + out[g] for g in G]
    u = x

    yo = [_dot3(pr[g], jnp.concatenate([u[g] * m_a, v_a[g], u[g] * m_b, v_b[g]], axis=0)) for g in G]
    y = [r_s[g] + yo[g] for g in G]

    brow = lax.broadcasted_iota(jnp.int32, (2 * nh, 2 * nh), 0) < nh
    bcol = lax.broadcasted_iota(jnp.int32, (2 * nh, 2 * nh), 1) < nh
    upd = [_dot3_tn(jnp.concatenate([u[g], v[g]], axis=0), jnp.concatenate([bh[g], kh[g]], axis=0)) for g in G]
    bd_new = [bd[g] * jnp.exp(cum[g][C - 1:C, :]) + jnp.where(brow == bcol, upd[g], 0.0) for g in G]
    return y, bd_new


def _wkv_chunk_kernel(r_ref, lw_ref, k_ref, v_ref, kk_ref, be_ref, s0_ref, y_ref, so_ref, bd_ref, *, nh, gp):
    c = pl.program_id(2)
    C = CHUNK
    w = 2 * nh

    @pl.when(c == 0)
    def _():
        bd_ref[...] = s0_ref[0]

    lw = lw_ref[...]
    k = k_ref[...]
    be = be_ref[...]
    cum = _cumsum_rows(lw)
    cl = cum[C - 1:C, :]
    wi = jnp.exp(-cum)
    we = jnp.exp(cl - cum)
    rb = r_ref[...] * jnp.exp(cum)
    kb = k * wi
    ab = -kk_ref[...] * jnp.exp(cum - lw)
    bb = be * wi
    bh = be * we
    kh = k * we
    v = v_ref[...]

    tiles = lambda t: [t[:, g * w:(g + 1) * w] for g in range(gp)]
    ys, bds = _wkv_pairs(tiles(rb), tiles(kb), tiles(ab), tiles(bb), tiles(bh), tiles(kh), tiles(v),
                         [bd_ref[g * w:(g + 1) * w, :] for g in range(gp)], tiles(cum), nh)
    y_ref[...] = ys[0] if gp == 1 else jnp.concatenate(ys, axis=1)
    bd_all = bds[0] if gp == 1 else jnp.concatenate(bds, axis=0)
    bd_ref[...] = bd_all

    @pl.when(c == pl.num_programs(2) - 1)
    def _():
        so_ref[0] = bd_all


def wkv_chunk(r, lw, k, v, kk, be, s0_bd, batch, seq, nh):
    dr = r.shape[1]
    w = 2 * nh
    pairs = dr // w
    gp = _pick(pairs, 16, 1)
    nc = seq // CHUNK
    tile = pl.BlockSpec((CHUNK, gp * w), lambda b, h, c: (b * nc + c, h))
    st = pl.BlockSpec((1, gp * w, w), lambda b, h, c: (b, h, 0))
    return pl.pallas_call(
        functools.partial(_wkv_chunk_kernel, nh=nh, gp=gp),
        grid=(batch, pairs // gp, nc),
        in_specs=[tile] * 6 + [st],
        out_specs=[tile, st],
        out_shape=[jax.ShapeDtypeStruct((batch * seq, dr), F32),
                   jax.ShapeDtypeStruct(s0_bd.shape, F32)],
        scratch_shapes=[pltpu.VMEM((gp * w, w), F32)],
        compiler_params=_cparams("parallel", "parallel", "arbitrary"),
        name="wkv7_chunk",
    )(r, lw, k, v, kk, be, s0_bd)


def _wkv_step_kernel(r_ref, lw_ref, k_ref, v_ref, kk_ref, be_ref, s_ref, *rest):
    *done_refs, y_ref, so_ref = rest
    s = s_ref[...]
    sa = jnp.sum(s * kk_ref[...], axis=2, keepdims=True)
    s_new = s * jnp.exp(lw_ref[...]) - sa * be_ref[...] + v_ref[...] * k_ref[...]
    y_ref[...] = jnp.sum(s_new * r_ref[...], axis=2, keepdims=True)
    _store_stacked(so_ref, done_refs, s_new)


def _store_stacked(so_ref, done_refs, s_new):
    if done_refs:
        for l, d_ref in enumerate(done_refs):
            so_ref[l] = d_ref[...]
        so_ref[len(done_refs)] = s_new
    else:
        so_ref[...] = s_new


def _state_specs(g, n, layer, done):
    st = pl.BlockSpec((g, n, n), lambda i: (i, 0, 0))
    st_in = pl.BlockSpec((None, g, n, n), lambda i: (layer, i, 0, 0))
    if done:
        return st_in, [st] * len(done), pl.BlockSpec((len(done) + 1, g, n, n), lambda i: (0, i, 0, 0))
    return st_in, [], st


def wkv_step(r, lw, k, v, kk, be, state, layer, done=()):
    _, h, n, _, b = state.shape
    g = _pick(h, 2, 1)
    row = pl.BlockSpec((g, 1, n, b), lambda i: (i, 0, 0, 0))
    col = pl.BlockSpec((g, n, 1, b), lambda i: (i, 0, 0, 0))
    st = pl.BlockSpec((g, n, n, b), lambda i: (i, 0, 0, 0))
    st_in = pl.BlockSpec((None, g, n, n, b), lambda i: (layer, i, 0, 0, 0))
    st_out = pl.BlockSpec((len(done) + 1, g, n, n, b), lambda i: (0, i, 0, 0, 0)) if done else st
    so_shape = ((len(done) + 1,) if done else ()) + (h, n, n, b)
    return pl.pallas_call(
        _wkv_step_kernel,
        grid=(h // g,),
        in_specs=[row, row, row, col, row, row, st_in] + [st] * len(done),
        out_specs=[col, st_out],
        out_shape=[jax.ShapeDtypeStruct((h, n, 1, b), F32), jax.ShapeDtypeStruct(so_shape, F32)],
        compiler_params=_cparams("parallel"),
        name="wkv7_step",
    )(r, lw, k, v, kk, be, state, *done)


def _rwkv_post_kernel(yp_ref, ys_ref, bo_ref, g_ref, w_ref, b_ref, bd_ref, o_ref, *, nh, prompt_tiles):
    y = jnp.where(pl.program_id(0) >= prompt_tiles, ys_ref[...], yp_ref[...])
    bd = bd_ref[...]
    mean = _head_sum(y, bd) * (1.0 / nh)
    d = y - mean
    var = _head_sum(d * d, bd) * (1.0 / nh)
    yn = d * lax.rsqrt(var + EPS_LNX) * w_ref[...] + b_ref[...]
    o_ref[...] = ((yn + bo_ref[...]) * g_ref[...]).astype(o_ref.dtype)


def rwkv_post(y_p, y_s, bonus, g, lnx_w, lnx_b, bd, nh):
    mp, dr = y_p.shape
    dec = y_s.shape[0]
    m = mp + dec
    tr = _pick(math.gcd(mp, dec), 256, 16)
    pt = mp // tr
    row = lambda i: (i, 0)
    fix = lambda i: (0, 0)
    return pl.pallas_call(
        functools.partial(_rwkv_post_kernel, nh=nh, prompt_tiles=pt),
        grid=(m // tr,),
        in_specs=[pl.BlockSpec((tr, dr), lambda i: (jnp.minimum(i, pt - 1), 0)),
                  pl.BlockSpec((tr, dr), lambda i: (jnp.maximum(i - pt, 0), 0)),
                  pl.BlockSpec((tr, dr), row), pl.BlockSpec((tr, dr), row),
                  pl.BlockSpec((1, dr), fix), pl.BlockSpec((1, dr), fix),
                  pl.BlockSpec((2 * LANES, LANES), fix)],
        out_specs=pl.BlockSpec((tr, dr), row),
        out_shape=jax.ShapeDtypeStruct((m, dr), BF16),
        compiler_params=_cparams("parallel"),
        name="rwkv_post",
    )(y_p, y_s, bonus, g, lnx_w.reshape(1, dr), lnx_b.reshape(1, dr), bd)


def _hgrn_gates(qh, fh, lb):
    f = lb + (1.0 - lb) * _sigmoid(fh)
    logf = jnp.log(jnp.maximum(f, MIN_FORGET))
    kin = (1.0 - lb) * _sigmoid(-fh)
    q = qh * _sigmoid(qh)
    return q, kin, logf


def _hgrn_out(o, gh, nw):
    o = o * lax.rsqrt(jnp.mean(o * o, axis=-1, keepdims=True) + EPS_HGRN)
    return o * nw * (gh * _sigmoid(gh))


def _hgrn_heads(q, kin, v, b, st):
    C = CHUNK
    H = range(len(q))
    o_inter = [_dot3_nt(q[h] * jnp.exp(b[h]), st[h]) for h in H]

    row = lax.broadcasted_iota(jnp.int32, (C, C), 0)
    col = lax.broadcasted_iota(jnp.int32, (C, C), 1)
    rsub = row % SUB
    scol = lax.broadcasted_iota(jnp.int32, (SUB, C), 1)
    offs = []
    for blk in range(1, C // SUB):
        lo = blk * SUB
        offs.append([_dot3_nt(q[h][lo:lo + SUB] * jnp.exp(b[h][lo:lo + SUB] - b[h][lo - 1:lo]),
                              kin[h] * jnp.exp(jnp.minimum(b[h][lo - 1:lo] - b[h], 0.0))) for h in H])
    a = []
    for h in H:
        diag = jnp.zeros((C, C), F32)
        bk = b[h] - jnp.log(kin[h])
        for dlt in range(SUB):
            if dlt == 0:
                w = q[h] * kin[h]
            else:
                w = q[h] * jnp.exp(jnp.minimum(b[h] - pltpu.roll(bk, dlt, axis=0), 0.0))
            colsum = jnp.sum(w, axis=-1, keepdims=True)
            diag = jnp.where((col == row - dlt) & (rsub >= dlt), colsum, diag)
        blocks = [diag[0:SUB]]
        for blk in range(1, C // SUB):
            lo = blk * SUB
            blocks.append(jnp.where(scol < lo, offs[blk - 1][h], diag[lo:lo + SUB]))
        a.append(jnp.concatenate(blocks, axis=0))
    o_intra = [_dot3(a[h], v[h]) for h in H]
    upd = [_dot3_tn(v[h], kin[h] * jnp.exp(b[h][C - 1:C, :] - b[h])) for h in H]
    o = [o_inter[h] + o_intra[h] for h in H]
    st_new = [st[h] * jnp.exp(b[h][C - 1:C, :]) + upd[h] for h in H]
    return o, st_new


def _hgrn_chunk_kernel(q_ref, f_ref, i_ref, g_ref, lb_ref, nw_ref, s0_ref, o_ref, so_ref, st_ref, *, dh, gh):
    c = pl.program_id(2)

    @pl.when(c == 0)
    def _():
        for h in range(gh):
            st_ref[h * dh:(h + 1) * dh, :] = s0_ref[0, h].T

    q, kin, logf = _hgrn_gates(q_ref[...], f_ref[...], lb_ref[...])
    v = i_ref[...]
    b = _cumsum_rows(logf)
    tiles = lambda t: [t[:, h * dh:(h + 1) * dh] for h in range(gh)]
    os, sts = _hgrn_heads(tiles(q), tiles(kin), tiles(v), tiles(b),
                          [st_ref[h * dh:(h + 1) * dh, :] for h in range(gh)])
    os = [o * lax.rsqrt(jnp.mean(o * o, axis=-1, keepdims=True) + EPS_HGRN) for o in os]
    o = os[0] if gh == 1 else jnp.concatenate(os, axis=1)
    gate = g_ref[...]
    o_ref[...] = (o * nw_ref[...] * (gate * _sigmoid(gate))).astype(o_ref.dtype)
    st_ref[...] = sts[0] if gh == 1 else jnp.concatenate(sts, axis=0)

    @pl.when(c == pl.num_programs(2) - 1)
    def _():
        for h in range(gh):
            so_ref[0, h] = sts[h].T


def hgrn_chunk(proj, lb, nw, s0, batch, seq, heads, dh):
    nc = seq // CHUNK
    gh = _pick(heads, 8, 1)
    wide = gh * dh
    hb = heads // gh

    def seg(j):
        return pl.BlockSpec((CHUNK, wide), lambda b, h, c: (b * nc + c, j * hb + h))

    vec = pl.BlockSpec((1, wide), lambda b, h, c: (0, h))
    st = pl.BlockSpec((1, gh, dh, dh), lambda b, h, c: (b, h, 0, 0))
    return pl.pallas_call(
        functools.partial(_hgrn_chunk_kernel, dh=dh, gh=gh),
        grid=(batch, hb, nc),
        in_specs=[seg(0), seg(1), seg(2), seg(3), vec, vec, st],
        out_specs=[pl.BlockSpec((CHUNK, wide), lambda b, h, c: (b * nc + c, h)), st],
        out_shape=[jax.ShapeDtypeStruct((batch * seq, heads * dh), BF16),
                   jax.ShapeDtypeStruct(s0.shape, F32)],
        scratch_shapes=[pltpu.VMEM((wide, dh), F32)],
        compiler_params=_cparams("parallel", "parallel", "arbitrary"),
        name="hgrn2_chunk",
    )(proj, proj, proj, proj, lb.reshape(1, -1), nw.reshape(1, -1), s0)


def _hgrn_step_kernel(q_ref, f_ref, i_ref, g_ref, lb_ref, nw_ref, s_ref, *rest):
    *done_refs, o_ref, so_ref = rest
    s = s_ref[...]
    n = s.shape[-1]
    q, kin, logf = _hgrn_gates(q_ref[...], f_ref[...], lb_ref[...])
    eye = (lax.broadcasted_iota(jnp.int32, (n, n), 0) == lax.broadcasted_iota(jnp.int32, (n, n), 1))[None]

    def col(x):
        return jnp.sum(jnp.where(eye, x, 0.0), axis=-1, keepdims=True)

    s_new = s * col(jnp.exp(logf)) + col(kin) * i_ref[...]
    o = jnp.sum(s_new * col(q), axis=-2, keepdims=True)
    o_ref[...] = _hgrn_out(o, g_ref[...], nw_ref[...]).astype(o_ref.dtype)
    _store_stacked(so_ref, done_refs, s_new)


def hgrn_step(qh, fh, ih, gh, lb, nw, state, layer, heads, done=()):
    _, bh, n, _ = state.shape
    g = _pick(heads, 16, 1)
    hb = heads // g
    vec = pl.BlockSpec((g, 1, n), lambda i: (i, 0, 0))
    par = pl.BlockSpec((g, 1, n), lambda i: (i % hb, 0, 0))
    st_in, done_specs, st_out = _state_specs(g, n, layer, done)
    so_shape = (len(done) + 1, bh, n, n) if done else (bh, n, n)
    return pl.pallas_call(
        _hgrn_step_kernel,
        grid=(bh // g,),
        in_specs=[vec] * 4 + [par, par, st_in] + done_specs,
        out_specs=[vec, st_out],
        out_shape=[jax.ShapeDtypeStruct((bh, 1, n), BF16), jax.ShapeDtypeStruct(so_shape, F32)],
        compiler_params=_cparams("parallel"),
        name="hgrn2_step",
    )(qh, fh, ih, gh, lb, nw, state, *done)


def _window(lo, hi):
    return (lo // LANES) * LANES, _round_up(hi, LANES)


def _window_rows(w, lo, hi):
    ws, we = _window(lo, hi)
    return jnp.pad(w, ((lo - ws, we - hi), (0, 0)))


def kernel(x_prompt, x_sample, state_wkv, state_hgrn, state_shift, norm_mix, w_in, mu_shift, w0, w2, a0, a2, v0, v1, v2, g2, k_k, k_a, r_k, lnx_w, lnx_b, hgrn_lb, hgrn_norm, w_out, norm_ffn, w_gate, w_up, w_down, norm_final):
    batch, seq, dm = x_prompt.shape
    dec = x_sample.shape[0]
    depth = w_in.shape[0]
    _, _, rh, nh, _ = state_wkv.shape
    _, _, hh, dh, _ = state_hgrn.shape
    dr, dhg = rh * nh, hh * dh
    rw, ra, rg = w2.shape[1], a2.shape[1], g2.shape[1]
    rp = 3 * dr + rw + ra + rg
    n_in = rp + 4 * dhg
    assert x_sample.shape[1] == 1 and seq % CHUNK == 0 and CHUNK % SUB == 0
    assert 2 * nh == LANES and dh == LANES and rh % 2 == 0 and dr % LANES == 0
    assert w_in.shape[2] == n_in and state_shift.shape[2] == rp
    seg_w = (3 * dr, 3 * dr + rw)
    seg_a = (seg_w[1], seg_w[1] + ra)
    seg_g = (seg_a[1], rp)
    rpp = _round_up(rp, LANES)
    mp = batch * seq
    dims = (dr, (_window(*seg_w), _window(*seg_a), _window(*seg_g)), rpp, batch, seq)
    pad_rp = lambda t: jnp.pad(t, [(0, 0)] * (t.ndim - 1) + [(0, rpp - rp)])

    pz = jax.nn.softmax(hgrn_lb.astype(F32), axis=0)
    lbs = jnp.cumsum(pz, axis=0) - pz[0:1]

    lane_head = jnp.arange(LANES) // nh
    bd = (lane_head[:, None] == lane_head[None, :]).astype(BF16)
    bd = jnp.concatenate([bd, bd], axis=0)
    w_in_t = jnp.swapaxes(w_in, 1, 2)
    w_r_t = jnp.pad(w_in_t[:, :rp], ((0, 0), (0, rpp - rp), (0, 0))).astype(BF16)
    w_h_t = w_in_t[:, rp:].astype(BF16)
    w_down_bf = w_down.astype(BF16)
    state_wkv_t = jnp.transpose(state_wkv, (0, 2, 3, 4, 1))
    state_hgrn_f = state_hgrn.reshape(depth, dec * hh, dh, dh)

    x = jnp.concatenate([x_prompt.reshape(mp, dm), x_sample.reshape(dec, dm)], axis=0)
    zero_bd = jnp.zeros((batch, (rh // 2) * LANES, LANES), F32)
    zero_hg = jnp.zeros((batch, hh, dh, dh), F32)
    zero_shift = jnp.zeros((batch, 1, rpp), F32)

    wkv_p, hgrn_p, shift_p, wkv_s, hgrn_s, shift_s = [], [], [], [], [], []
    v_first = None
    for l in range(depth):
        h = rmsnorm(x, norm_mix[l], BF16)
        proj = matmul_nt(h, w_r_t, l, F32)
        proj_h = matmul_nt(h, w_h_t, l, F32)

        vgate = None if l == 0 else (v_first, v0[l - 1], _r3_rows(v1[l - 1]), _r3_rows(v2[l - 1]))
        r, lw, k, v, kk, be, g, bonus = rwkv_prep(
            proj, zero_shift, pad_rp(state_shift[l]), pad_rp(mu_shift[l]), w0[l],
            _r3_rows(_window_rows(w2[l], *seg_w)), a0[l], _r3_rows(_window_rows(a2[l], *seg_a)),
            _r3_rows(_window_rows(g2[l], *seg_g)), k_k[l], k_a[l], r_k[l].reshape(-1), bd, vgate, dims)
        if l == 0:
            v_first = v
        y_p, bd_out = wkv_chunk(r, lw, k, v, kk, be, zero_bd, batch, seq, nh)
        rows = lambda t: t[mp:].T.reshape(rh, 1, nh, dec)
        y_s, wkv_new_s = wkv_step(rows(r), rows(lw), rows(k), v[mp:].T.reshape(rh, nh, 1, dec), rows(kk), rows(be),
                                  state_wkv_t, l, wkv_s if l == depth - 1 else ())
        y_rwkv = rwkv_post(y_p, y_s.reshape(dr, dec).T, bonus, g, lnx_w[l], lnx_b[l], bd, nh)
        bd4 = bd_out.reshape(batch, rh // 2, LANES, LANES)
        wkv_p.append(jnp.stack([bd4[:, :, :nh, :nh], bd4[:, :, nh:, nh:]], axis=2).reshape(batch, rh, nh, nh))
        wkv_s.append(wkv_new_s)
        shift_p.append(jnp.concatenate([proj[(b + 1) * seq - 1:(b + 1) * seq, :rp] for b in range(batch)], axis=0))
        shift_s.append(proj[mp:, :rp])

        o_p, hg_out = hgrn_chunk(proj_h, lbs[l], hgrn_norm[l], zero_hg, batch, seq, hh, dh)
        ph_s = proj_h[mp:].reshape(dec, 4, hh, dh).transpose(1, 0, 2, 3).reshape(4, dec * hh, 1, dh)
        o_s, hgrn_new_s = hgrn_step(ph_s[0], ph_s[1], ph_s[2], ph_s[3], lbs[l].reshape(hh, 1, dh),
                                    hgrn_norm[l].reshape(hh, 1, dh), state_hgrn_f, l, hh,
                                    hgrn_s if l == depth - 1 else ())
        y_hgrn = jnp.concatenate([o_p, o_s.reshape(dec, dhg)], axis=0)
        hgrn_p.append(hg_out)
        hgrn_s.append(hgrn_new_s)

        x = matmul2_residual(y_rwkv, y_hgrn, w_out, l, x)

        h = rmsnorm(x, norm_ffn[l], BF16)
        act = gate_up(h, w_gate, w_up, l)
        x = matmul_residual(act, w_down_bf, l, x)

    y_p = rmsnorm(x, norm_final, F32, 0, mp)
    y_s = rmsnorm(x, norm_final, F32, mp, dec)
    stacked = lambda t, shape: (t[-1] if depth > 1 else t[0][None]).reshape((depth,) + shape)
    wkv_s_out = jnp.transpose(stacked(wkv_s, (rh, nh, nh, dec)), (0, 4, 1, 2, 3))
    return (y_p.reshape(batch, seq, dm), y_s.reshape(dec, 1, dm),
            jnp.stack(wkv_p), jnp.stack(hgrn_p), jnp.stack(shift_p),
            wkv_s_out, stacked(hgrn_s, (dec, hh, dh, dh)), jnp.stack(shift_s))
```
